```python
import math, functools
import jax, jax.numpy as jnp
from jax import lax
import numpy as np

D_MODEL = 1024
BATCH = 16
SEQ = 2048
DEPTH = 1
DEC_BATCH = 32
DEC_SEQ = 1
PAST_LEN = 16384
PAGE_SIZE = 128

D_MIX = D_MODEL
D_ATTN = D_MIX // 2
D_CONV = D_MIX - D_ATTN
HEAD_DIM = 64
N_HEADS = D_ATTN // HEAD_DIM
CONV_W = 3
MOBA_BLOCK = 256
MOBA_TOPK = 3
D_FF = 2816
FFN_CONV_W = 3
Q_CHUNK = 16
LN_EPS = 1e-5
DEEPNORM_ALPHA = (2.0 * DEPTH) ** 0.25
DEEPNORM_BETA = (8.0 * DEPTH) ** -0.25
D_IN_PROJ = 3 * D_ATTN + 3 * D_CONV
SPLITS = (D_ATTN, 2 * D_ATTN, 3 * D_ATTN, 3 * D_ATTN + D_CONV, 3 * D_ATTN + 2 * D_CONV)

kernel_name = "hymba_moba_shortconv_convffn_deepnorm_step"


def alibi_slopes():
    return jnp.asarray(2.0 ** (-8.0 * np.arange(1, N_HEADS + 1) / N_HEADS), jnp.float32)


def layer_norm(x, g, b):
    xf = x.astype(jnp.float32)
    mu = jnp.mean(xf, axis=-1, keepdims=True)
    var = jnp.mean(jnp.square(xf - mu), axis=-1, keepdims=True)
    return ((xf - mu) * lax.rsqrt(var + LN_EPS) * g + b).astype(x.dtype)


def causal_dwconv(u, buf, w):
    t = u.shape[1]
    width = w.shape[0]
    ext = jnp.concatenate([buf.astype(u.dtype), u], axis=1)
    y = ext[:, 0:t] * w[0]
    for j in range(1, width):
        y = y + ext[:, j:j + t] * w[j]
    return y, ext[:, t:]


def moba_attend(q, t, kmean, fetch, slopes):
    nb = kmean.shape[1]
    k_sel = min(MOBA_TOPK, nb)
    own = t // MOBA_BLOCK
    gate = jnp.einsum('bthd,bnhd->bthn', q.astype(jnp.float32), kmean.astype(jnp.float32))
    past = jnp.arange(nb)[None, :] < own[:, None]
    gate = jnp.where(past[None, :, None, :], gate, -jnp.inf)
    _, sel = lax.top_k(gate, k_sel)
    n_past = jnp.minimum(own, MOBA_TOPK)
    sel_ok = jnp.arange(k_sel)[None, :] < n_past[:, None]
    one = sel.shape[:3] + (1,)
    blocks = jnp.concatenate([sel, jnp.broadcast_to(own[None, :, None, None], one)], axis=-1)
    ok = jnp.concatenate([jnp.broadcast_to(sel_ok[None, :, None, :], sel.shape),
                          jnp.ones(one, dtype=bool)], axis=-1)
    pos = blocks[..., None] * MOBA_BLOCK + jnp.arange(MOBA_BLOCK, dtype=blocks.dtype)
    valid = ok[..., None] & (pos <= t[None, :, None, None, None])
    pos = pos.reshape(pos.shape[:3] + (-1,))
    valid = valid.reshape(pos.shape)
    k_g, v_g = fetch(pos)
    s = jnp.einsum('bthd,bthnd->bthn', q, k_g, preferred_element_type=jnp.float32)
    s = s / math.sqrt(HEAD_DIM)
    dist = (t[None, :, None, None] - pos).astype(jnp.float32)
    s = s - slopes[None, None, :, None] * dist
    s = jnp.where(valid, s, -jnp.inf)
    p = jax.nn.softmax(s, axis=-1)
    return jnp.einsum('bthn,bthnd->bthd', p.astype(v_g.dtype), v_g)


def prompt_attend(q, k, v, slopes):
    bsz, s = q.shape[:2]
    nb = -(-s // MOBA_BLOCK)
    kpad = jnp.pad(k.astype(jnp.float32), ((0, 0), (0, nb * MOBA_BLOCK - s), (0, 0), (0, 0)))
    kmean = jnp.mean(kpad.reshape(bsz, nb, MOBA_BLOCK, N_HEADS, HEAD_DIM), axis=2)
    b_idx = jnp.arange(bsz)[:, None, None, None]
    h_idx = jnp.arange(N_HEADS)[None, None, :, None]

    def fetch(pos):
        pc = jnp.minimum(pos, s - 1)
        return k[b_idx, pc, h_idx], v[b_idx, pc, h_idx]

    nq = s // Q_CHUNK
    qc = q.reshape(bsz, nq, Q_CHUNK, N_HEADS, HEAD_DIM).transpose(1, 0, 2, 3, 4)
    tc = jnp.arange(s, dtype=jnp.int32).reshape(nq, Q_CHUNK)
    out = lax.map(lambda a: moba_attend(a[0], a[1], kmean, fetch, slopes), (qc, tc))
    return out.transpose(1, 0, 2, 3, 4).reshape(bsz, s, N_HEADS, HEAD_DIM)


def sample_attend(q, k, v, layer_idx, cache_k, cache_v, page_table, slopes):
    bsz, t = q.shape[:2]
    n_pages = PAST_LEN // PAGE_SIZE
    pages_per_block = MOBA_BLOCK // PAGE_SIZE
    total = PAST_LEN + t
    nb = -(-total // MOBA_BLOCK)
    pos_q = PAST_LEN + jnp.arange(t, dtype=jnp.int32)
    page_sum = jnp.sum(cache_k[layer_idx, page_table].astype(jnp.float32), axis=2)
    page_sum = jnp.pad(page_sum, ((0, 0), (0, nb * pages_per_block - n_pages), (0, 0), (0, 0)))
    page_sum = page_sum.at[:, pos_q // PAGE_SIZE].add(k.astype(jnp.float32))
    kmean = jnp.sum(page_sum.reshape(bsz, nb, pages_per_block, N_HEADS, HEAD_DIM), axis=2) / MOBA_BLOCK
    b_idx = jnp.arange(bsz)[:, None, None, None]
    h_idx = jnp.arange(N_HEADS)[None, None, :, None]

    def fetch(pos):
        is_past = (pos < PAST_LEN)[..., None]
        pp = jnp.clip(pos, 0, PAST_LEN - 1)
        page = page_table[b_idx, pp // PAGE_SIZE]
        off = pp % PAGE_SIZE
        pn = jnp.clip(pos - PAST_LEN, 0, t - 1)
        k_g = jnp.where(is_past, cache_k[layer_idx, page, off, h_idx].astype(k.dtype), k[b_idx, pn, h_idx])
        v_g = jnp.where(is_past, cache_v[layer_idx, page, off, h_idx].astype(v.dtype), v[b_idx, pn, h_idx])
        return k_g, v_g

    return moba_attend(q, pos_q, kmean, fetch, slopes)


def layer(x, attend, conv_buf, ffn_buf, w_in, w_conv, w_out, ln1_g, ln1_b,
          w_up, w_fconv, w_down, ln2_g, ln2_b):
    bsz, t = x.shape[:2]
    z = x @ w_in
    q, k, v, gb, gc, h = jnp.split(z, SPLITS, axis=-1)
    q = q.reshape(bsz, t, N_HEADS, HEAD_DIM)
    k = k.reshape(bsz, t, N_HEADS, HEAD_DIM)
    v = v.reshape(bsz, t, N_HEADS, HEAD_DIM)
    att = attend(q, k, v).reshape(bsz, t, D_ATTN)
    conv_y, conv_buf_new = causal_dwconv(gc * h, conv_buf, w_conv)
    mix = jnp.concatenate([att, gb * conv_y], axis=-1) @ w_out
    x1 = layer_norm(DEEPNORM_ALPHA * x + mix, ln1_g, ln1_b)
    up, ffn_buf_new = causal_dwconv(x1 @ w_up, ffn_buf, w_fconv)
    a, g = jnp.split(up, 2, axis=-1)
    f = (jax.nn.silu(a) * g) @ w_down
    y = layer_norm(DEEPNORM_ALPHA * x1 + f, ln2_g, ln2_b)
    return y, k, v, conv_buf_new, ffn_buf_new


def setup_inputs(seed: int = 0) -> dict:
    key = jax.random.key(seed)
    ks = jax.random.split(key, 20)
    n_pages = PAST_LEN // PAGE_SIZE
    n_phys = (DEC_BATCH * n_pages * 5) // 4
    f32 = jnp.float32
    x_prompt = jax.random.normal(ks[0], (BATCH, SEQ, D_MODEL), f32)
    x_sample = jax.random.normal(ks[1], (DEC_BATCH, DEC_SEQ, D_MODEL), f32)
    cache_k = jax.random.normal(ks[2], (DEPTH, n_phys, PAGE_SIZE, N_HEADS, HEAD_DIM), f32)
    cache_v = jax.random.normal(ks[3], (DEPTH, n_phys, PAGE_SIZE, N_HEADS, HEAD_DIM), f32) * DEEPNORM_BETA
    state_conv = jax.random.normal(ks[4], (DEPTH, DEC_BATCH, CONV_W - 1, D_CONV), f32)
    state_ffn_conv = jax.random.normal(ks[5], (DEPTH, DEC_BATCH, FFN_CONV_W - 1, 2 * D_FF), f32) * DEEPNORM_BETA
    page_table = jax.random.permutation(ks[6], n_phys)[:DEC_BATCH * n_pages].reshape(DEC_BATCH, n_pages).astype(jnp.int32)
    col_scale = jnp.concatenate([jnp.ones((2 * D_ATTN,), f32), jnp.full((D_ATTN,), DEEPNORM_BETA, f32),
                                 jnp.ones((3 * D_CONV,), f32)])
    w_in = jax.random.normal(ks[7], (DEPTH, D_MODEL, D_IN_PROJ), f32) * D_MODEL ** -0.5 * col_scale
    w_conv = jax.random.normal(ks[8], (DEPTH, CONV_W, D_CONV), f32) * CONV_W ** -0.5
    w_out = jax.random.normal(ks[9], (DEPTH, D_MIX, D_MODEL), f32) * D_MIX ** -0.5 * DEEPNORM_BETA
    ln1_g = 1.0 + 0.02 * jax.random.normal(ks[10], (DEPTH, D_MODEL), f32)
    ln1_b = 0.02 * jax.random.normal(ks[11], (DEPTH, D_MODEL), f32)
    w_ffn_up = jax.random.normal(ks[12], (DEPTH, D_MODEL, 2 * D_FF), f32) * D_MODEL ** -0.5 * DEEPNORM_BETA
    w_ffn_conv = jax.random.normal(ks[13], (DEPTH, FFN_CONV_W, 2 * D_FF), f32) * FFN_CONV_W ** -0.5
    w_ffn_down = jax.random.normal(ks[14], (DEPTH, D_FF, D_MODEL), f32) * D_FF ** -0.5 * DEEPNORM_BETA
    ln2_g = 1.0 + 0.02 * jax.random.normal(ks[15], (DEPTH, D_MODEL), f32)
    ln2_b = 0.02 * jax.random.normal(ks[16], (DEPTH, D_MODEL), f32)
    return {"x_prompt": x_prompt, "x_sample": x_sample, "cache_k": cache_k, "cache_v": cache_v,
            "state_conv": state_conv, "state_ffn_conv": state_ffn_conv, "page_table": page_table,
            "w_in": w_in, "w_conv": w_conv, "w_out": w_out, "ln1_g": ln1_g, "ln1_b": ln1_b,
            "w_ffn_up": w_ffn_up, "w_ffn_conv": w_ffn_conv, "w_ffn_down": w_ffn_down,
            "ln2_g": ln2_g, "ln2_b": ln2_b}


def reference(x_prompt, x_sample, cache_k, cache_v, state_conv, state_ffn_conv, page_table,
              w_in, w_conv, w_out, ln1_g, ln1_b, w_ffn_up, w_ffn_conv, w_ffn_down, ln2_g, ln2_b):
    slopes = alibi_slopes()
    bsz = x_prompt.shape[0]
    yp, ys = x_prompt, x_sample
    kp_l, vp_l, cp_l, fp_l, ks_l, vs_l, cs_l, fs_l = [], [], [], [], [], [], [], []
    for l in range(DEPTH):
        params = (w_in[l], w_conv[l], w_out[l], ln1_g[l], ln1_b[l],
                  w_ffn_up[l], w_ffn_conv[l], w_ffn_down[l], ln2_g[l], ln2_b[l])
        attend_p = functools.partial(prompt_attend, slopes=slopes)
        conv0 = jnp.zeros((bsz, CONV_W - 1, D_CONV), x_prompt.dtype)
        ffn0 = jnp.zeros((bsz, FFN_CONV_W - 1, 2 * D_FF), x_prompt.dtype)
        yp, kp, vp, cp, fp = layer(yp, attend_p, conv0, ffn0, *params)
        attend_s = functools.partial(sample_attend, layer_idx=l, cache_k=cache_k, cache_v=cache_v,
                                     page_table=page_table, slopes=slopes)
        ys, ks_, vs_, cs_, fs_ = layer(ys, attend_s, state_conv[l], state_ffn_conv[l], *params)
        kp_l.append(kp); vp_l.append(vp); cp_l.append(cp); fp_l.append(fp)
        ks_l.append(ks_); vs_l.append(vs_); cs_l.append(cs_); fs_l.append(fs_)
    k_prompt = jnp.stack(kp_l); v_prompt = jnp.stack(vp_l)
    conv_prompt = jnp.stack(cp_l); ffn_conv_prompt = jnp.stack(fp_l)
    k_sample = jnp.stack(ks_l); v_sample = jnp.stack(vs_l)
    conv_sample = jnp.stack(cs_l); ffn_conv_sample = jnp.stack(fs_l)
    return (yp, ys, k_prompt, v_prompt, conv_prompt, ffn_conv_prompt,
            k_sample, v_sample, conv_sample, ffn_conv_sample)
```

```python
import functools

import jax
import jax.numpy as jnp
from jax import lax
from jax.experimental import pallas as pl
from jax.experimental.pallas import tpu as pltpu

F32 = jnp.float32
BF16 = jnp.bfloat16

HEAD_DIM = 64
MOBA_BLOCK = 256
MOBA_TOPK = 3
CONV_W = 3
LN_EPS = 1e-5
NEG = -1e30
SUBLANES = 8
LANES = 128
MIB = 1024 * 1024

_NT = (((1,), (1,)), ((), ()))


def _dot(a, b):
    return jnp.dot(a, b, preferred_element_type=F32)


def _dot_nt(a, b):
    return lax.dot_general(a, b, _NT, preferred_element_type=F32)


def _layer_norm(x, g, b):
    mu = jnp.mean(x, axis=-1, keepdims=True)
    xc = x - mu
    var = jnp.mean(xc * xc, axis=-1, keepdims=True)
    return xc * lax.rsqrt(var + LN_EPS) * g + b


def _silu(a):
    return a * jax.nn.sigmoid(a)


def _inproj_kernel(x_ref, wt_ref, wr_ref, wc_ref,
                   qt_ref, kt_ref, vt_ref, krm_ref, kmean_ref, cm_ref, cst_ref,
                   ubuf_ref, *, tm, tiles_per_batch, d_attn, d_conv):
    t = pl.program_id(0)
    tb = t % tiles_per_batch
    xb = x_ref[...].astype(BF16)

    qt_ref[0] = (_dot_nt(wt_ref[0:d_attn, :], xb) * 0.125).astype(BF16)
    kt_ref[0] = _dot_nt(wt_ref[d_attn:2 * d_attn, :], xb)
    vt_ref[0] = _dot_nt(wt_ref[2 * d_attn:3 * d_attn, :], xb)

    k = _dot(xb, wr_ref[:, 0:d_attn])
    krm_ref[...] = k.astype(BF16)
    for blk in range(tm // MOBA_BLOCK):
        ksum = jnp.sum(k[blk * MOBA_BLOCK:(blk + 1) * MOBA_BLOCK, :], axis=0, keepdims=True)
        kmean_ref[0, pl.ds(tb * (tm // MOBA_BLOCK) + blk, 1), :] = ksum * (1.0 / MOBA_BLOCK)

    gb = _dot(xb, wr_ref[:, d_attn:d_attn + d_conv])
    gc = _dot(xb, wr_ref[:, d_attn + d_conv:d_attn + 2 * d_conv])
    hh = _dot(xb, wr_ref[:, d_attn + 2 * d_conv:d_attn + 3 * d_conv])
    u = gc * hh

    @pl.when(tb == 0)
    def _():
        ubuf_ref[0:SUBLANES, :] = jnp.zeros((SUBLANES, d_conv), F32)

    ubuf_ref[SUBLANES:SUBLANES + tm, :] = u
    wc = wc_ref[...]
    conv = (wc[0:1, :] * ubuf_ref[SUBLANES - 2:SUBLANES - 2 + tm, :]
            + wc[1:2, :] * ubuf_ref[SUBLANES - 1:SUBLANES - 1 + tm, :]
            + wc[2:3, :] * u)
    cm_ref[...] = (gb * conv).astype(BF16)
    cst_ref[0] = ubuf_ref[SUBLANES + tm - (CONV_W - 1):SUBLANES + tm, :]
    ubuf_ref[0:SUBLANES, :] = ubuf_ref[tm:tm + SUBLANES, :]


def _prompt_inproj(x2d, wt, wr, w_conv, *, bsz, seq, tm):
    n, d_model = x2d.shape
    d_attn = wt.shape[0] // 3
    d_conv = (wr.shape[1] - d_attn) // 3
    tpb = seq // tm
    nb = seq // MOBA_BLOCK
    kern = functools.partial(_inproj_kernel, tm=tm, tiles_per_batch=tpb, d_attn=d_attn, d_conv=d_conv)
    const = lambda t: (0, 0)
    ct_spec = lambda: pl.BlockSpec((1, d_attn, tm), lambda t: (t // tpb, 0, t % tpb))
    return pl.pallas_call(
        kern,
        grid=(n // tm,),
        in_specs=[
            pl.BlockSpec((tm, d_model), lambda t: (t, 0)),
            pl.BlockSpec(wt.shape, const),
            pl.BlockSpec(wr.shape, const),
            pl.BlockSpec(w_conv.shape, const),
        ],
        out_specs=[
            ct_spec(), ct_spec(), ct_spec(),
            pl.BlockSpec((tm, d_attn), lambda t: (t, 0)),
            pl.BlockSpec((1, nb, d_attn), lambda t: (t // tpb, 0, 0)),
            pl.BlockSpec((tm, d_conv), lambda t: (t, 0)),
            pl.BlockSpec((1, CONV_W - 1, d_conv), lambda t: (t // tpb, 0, 0)),
        ],
        out_shape=[
            jax.ShapeDtypeStruct((bsz, d_attn, seq), BF16),
            jax.ShapeDtypeStruct((bsz, d_attn, seq), F32),
            jax.ShapeDtypeStruct((bsz, d_attn, seq), F32),
            jax.ShapeDtypeStruct((n, d_attn), BF16),
            jax.ShapeDtypeStruct((bsz, nb, d_attn), F32),
            jax.ShapeDtypeStruct((n, d_conv), BF16),
            jax.ShapeDtypeStruct((bsz, CONV_W - 1, d_conv), F32),
        ],
        scratch_shapes=[pltpu.VMEM((tm + 2 * SUBLANES, d_conv), F32)],
        compiler_params=pltpu.CompilerParams(
            dimension_semantics=("arbitrary",), vmem_limit_bytes=48 * MIB),
        name="prompt_inproj",
    )(x2d, wt, wr, w_conv)


def _topk_bias(g, n_past):
    nb = g.shape[0]
    sub = lax.broadcasted_iota(jnp.int32, g.shape, 0)
    rank = jnp.zeros(g.shape, jnp.int32)
    for jp in range(nb):
        row = g[jp:jp + 1, :]
        beats = (row > g) | ((row == g) & (jp < sub))
        rank = rank + jnp.where(beats, 1, 0) * jnp.where(jp < n_past, 1, 0)
    sel = (sub < n_past) & (rank < MOBA_TOPK)
    return jnp.where(sel, 0.0, NEG).astype(F32)


def _attn_kernel(qt_ref, k_ref, vt_ref, kmean_ref, o_ref, vtb_ref, selb_ref, *, nb):
    pair = pl.program_id(1)
    i = pl.program_id(2)
    blk = MOBA_BLOCK

    @pl.when(i == 0)
    def _():
        for j in range(nb):
            vtb_ref[j] = vt_ref[0, :, j * blk:(j + 1) * blk].astype(BF16)

    slope_even = jnp.where(pair == 0, 2.0 ** -1,
                           jnp.where(pair == 1, 2.0 ** -3,
                                     jnp.where(pair == 2, 2.0 ** -5, 2.0 ** -7))).astype(F32)
    lane = lax.broadcasted_iota(jnp.int32, (blk, blk), 1)
    sub = lax.broadcasted_iota(jnp.int32, (blk, blk), 0)
    d0 = (lane - sub).astype(F32)
    causal = lane >= sub

    qt = qt_ref[0]
    zeros = jnp.zeros((HEAD_DIM, blk), BF16)
    kmean = kmean_ref[0]
    i_f = i.astype(F32)

    qpad, sd0, slope, init = [], [], [], []
    kd = k_ref[pl.ds(pl.multiple_of(i * blk, blk), blk), :]
    for e in range(2):
        sl = slope_even * (0.5 ** e)
        qh = qt[e * HEAD_DIM:(e + 1) * HEAD_DIM, :]
        qp = jnp.concatenate([qh, zeros] if e == 0 else [zeros, qh], axis=0)
        gate = jnp.dot(kmean[:, e * HEAD_DIM:(e + 1) * HEAD_DIM], qh.astype(F32),
                       preferred_element_type=F32, precision=lax.Precision.HIGHEST)
        selb_ref[e] = _topk_bias(gate, i)
        s0 = sl * d0
        s = jnp.where(causal, _dot(kd, qp) - s0, NEG)
        m = jnp.max(s, axis=0, keepdims=True)
        pr = jnp.exp(s - m)
        l = jnp.sum(pr, axis=0, keepdims=True)
        acc = _dot(vtb_ref[i, e * HEAD_DIM:(e + 1) * HEAD_DIM, :], pr.astype(BF16))
        qpad.append(qp); sd0.append(s0); slope.append(sl); init.append((m, l, acc))

    def body(j, carry):
        kj = k_ref[pl.ds(pl.multiple_of(j * blk, blk), blk), :]
        out = []
        for e in range(2):
            m, l, acc = carry[e]
            row = selb_ref[e, pl.ds(j, 1), :] - slope[e] * (blk * (i_f - j.astype(F32)))
            s = _dot(kj, qpad[e]) + row - sd0[e]
            m_new = jnp.maximum(m, jnp.max(s, axis=0, keepdims=True))
            alpha = jnp.exp(m - m_new)
            pr = jnp.exp(s - m_new)
            l = alpha * l + jnp.sum(pr, axis=0, keepdims=True)
            acc = alpha * acc + _dot(vtb_ref[j, e * HEAD_DIM:(e + 1) * HEAD_DIM, :], pr.astype(BF16))
            out.append((m_new, l, acc))
        return tuple(out)

    fin = lax.fori_loop(0, i, body, tuple(init))
    out_t = jnp.concatenate([fin[0][2] * (1.0 / fin[0][1]), fin[1][2] * (1.0 / fin[1][1])], axis=0)
    o_ref[...] = out_t.T.astype(BF16)


def _prompt_attention(qt, krm, vt, kmean, *, bsz, seq):
    d_attn = qt.shape[1]
    nb = seq // MOBA_BLOCK
    n_pairs = d_attn // (2 * HEAD_DIM)
    pw = 2 * HEAD_DIM
    kern = functools.partial(_attn_kernel, nb=nb)
    return pl.pallas_call(
        kern,
        grid=(bsz, n_pairs, nb),
        in_specs=[
            pl.BlockSpec((1, pw, MOBA_BLOCK), lambda b, p, i: (b, p, i)),
            pl.BlockSpec((seq, pw), lambda b, p, i: (b, p)),
            pl.BlockSpec((1, pw, seq), lambda b, p, i: (b, p, 0)),
            pl.BlockSpec((1, nb, pw), lambda b, p, i: (b, 0, p)),
        ],
        out_specs=pl.BlockSpec((MOBA_BLOCK, pw), lambda b, p, i: (b * nb + i, p)),
        out_shape=jax.ShapeDtypeStruct((bsz * seq, d_attn), BF16),
        scratch_shapes=[
            pltpu.VMEM((nb, pw, MOBA_BLOCK), BF16),
            pltpu.VMEM((2, nb, MOBA_BLOCK), F32),
        ],
        compiler_params=pltpu.CompilerParams(
            dimension_semantics=("arbitrary", "arbitrary", "arbitrary"), vmem_limit_bytes=32 * MIB),
        name="prompt_moba_attention",
    )(qt, krm, vt, kmean)


def _ffn_kernel(x_ref, att_ref, cm_ref, wo_ref, g1_ref, b1_ref, wup_ref, wfc_ref, wdn_ref, g2_ref, b2_ref,
                y_ref, fst_ref,
                x1_ref, x1b_ref, acc_ref, ua_ref, ug_ref, carry_ref,
                *, tm, tiles_per_batch, d_attn, d_ff, cw, alpha):
    t = pl.program_id(0)
    mix = _dot(att_ref[...], wo_ref[0:d_attn, :]) + _dot(cm_ref[...], wo_ref[d_attn:, :])
    x1 = _layer_norm(alpha * x_ref[...] + mix, g1_ref[...], b1_ref[...])
    x1_ref[...] = x1
    x1b_ref[...] = x1.astype(BF16)

    @pl.when(t % tiles_per_batch == 0)
    def _():
        carry_ref[...] = jnp.zeros(carry_ref.shape, F32)

    def up_conv(col0, ubuf):
        u = _dot(x1b_ref[...], wup_ref[:, col0:col0 + cw])
        ubuf[0:SUBLANES, :] = carry_ref[:, col0:col0 + cw]
        ubuf[SUBLANES:SUBLANES + tm, :] = u
        w = wfc_ref[:, col0:col0 + cw]
        conv = (w[0:1, :] * ubuf[SUBLANES - 2:SUBLANES - 2 + tm, :]
                + w[1:2, :] * ubuf[SUBLANES - 1:SUBLANES - 1 + tm, :]
                + w[2:3, :] * u)
        carry_ref[:, col0:col0 + cw] = ubuf[tm:tm + SUBLANES, :]
        fst_ref[0, :, col0:col0 + cw] = ubuf[SUBLANES + tm - (CONV_W - 1):SUBLANES + tm, :]
        return conv

    for c in range(d_ff // cw):
        a = up_conv(c * cw, ua_ref)
        g = up_conv(d_ff + c * cw, ug_ref)
        hcol = (_silu(a) * g).astype(BF16)
        part = _dot(hcol, wdn_ref[c * cw:(c + 1) * cw, :])
        if c == 0:
            acc_ref[...] = part
        else:
            acc_ref[...] += part

    y_ref[...] = _layer_norm(alpha * x1_ref[...] + acc_ref[...], g2_ref[...], b2_ref[...])


def _prompt_ffn(x2d, att, cm, wo, g1, b1, wup, wfc, wdn, g2, b2, *, bsz, seq, tm, cw, alpha):
    n, d_model = x2d.shape
    d_attn = att.shape[1]
    d_ff = wdn.shape[0]
    tpb = seq // tm
    kern = functools.partial(_ffn_kernel, tm=tm, tiles_per_batch=tpb, d_attn=d_attn, d_ff=d_ff, cw=cw,
                             alpha=alpha)
    const = lambda t: (0, 0)
    resident = lambda a: pl.BlockSpec(a.shape, const, pipeline_mode=pl.Buffered(1))
    row = lambda w: pl.BlockSpec((tm, w), lambda t: (t, 0))
    return pl.pallas_call(
        kern,
        grid=(n // tm,),
        in_specs=[row(d_model), row(d_attn), row(cm.shape[1]),
                  resident(wo), resident(g1), resident(b1), resident(wup), resident(wfc), resident(wdn),
                  resident(g2), resident(b2)],
        out_specs=[row(d_model),
                   pl.BlockSpec((1, CONV_W - 1, 2 * d_ff), lambda t: (t // tpb, 0, 0))],
        out_shape=[jax.ShapeDtypeStruct((n, d_model), F32),
                   jax.ShapeDtypeStruct((bsz, CONV_W - 1, 2 * d_ff), F32)],
        scratch_shapes=[
            pltpu.VMEM((tm, d_model), F32),
            pltpu.VMEM((tm, d_model), BF16),
            pltpu.VMEM((tm, d_model), F32),
            pltpu.VMEM((tm + 2 * SUBLANES, cw), F32),
            pltpu.VMEM((tm + 2 * SUBLANES, cw), F32),
            pltpu.VMEM((SUBLANES, 2 * d_ff), F32),
        ],
        compiler_params=pltpu.CompilerParams(
            dimension_semantics=("arbitrary",), vmem_limit_bytes=56 * MIB),
        name="prompt_outproj_ffn",
    )(x2d, att, cm, wo, g1, b1, wup, wfc, wdn, g2, b2)


def _sample_inproj_kernel(x_ref, w_ref, wc_ref, st0_ref, st1_ref,
                          q_ref, k_ref, v_ref, cm_ref, u_ref, *, d_attn, d_conv):
    xb = x_ref[...].astype(BF16)
    z = _dot(xb, w_ref[...])
    q_ref[...] = z[:, 0:d_attn] * 0.125
    k_ref[...] = z[:, d_attn:2 * d_attn]
    v_ref[...] = z[:, 2 * d_attn:3 * d_attn]
    o = 3 * d_attn
    gb = z[:, o:o + d_conv]
    u = z[:, o + d_conv:o + 2 * d_conv] * z[:, o + 2 * d_conv:o + 3 * d_conv]
    wc = wc_ref[...]
    conv = wc[0:1, :] * st0_ref[...] + wc[1:2, :] * st1_ref[...] + wc[2:3, :] * u
    cm_ref[...] = gb * conv
    u_ref[...] = u


def _sample_inproj(xs, w_in_b, w_conv, st0, st1, *, d_attn, d_conv):
    b = xs.shape[0]
    kern = functools.partial(_sample_inproj_kernel, d_attn=d_attn, d_conv=d_conv)
    sd = lambda w: jax.ShapeDtypeStruct((b, w), F32)
    return pl.pallas_call(
        kern,
        out_shape=[sd(d_attn), sd(d_attn), sd(d_attn), sd(d_conv), sd(d_conv)],
        compiler_params=pltpu.CompilerParams(vmem_limit_bytes=32 * MIB),
        name="sample_inproj",
    )(xs, w_in_b, w_conv, st0, st1)


def _page_copies(pt_ref, kc_ref, buf_ref, sem_ref, step, slot, *, chunks_per_seq, pages_per_chunk):
    b = step // chunks_per_seq
    c = step % chunks_per_seq
    copies = []
    for pg in range(pages_per_chunk):
        phys = pt_ref[b, c * pages_per_chunk + pg]
        copies.append(pltpu.make_async_copy(kc_ref.at[phys], buf_ref.at[slot, pg], sem_ref.at[slot]))
    return copies


def _sample_gate_kernel(pt_ref, kc_ref, qcol_ref, kcol_ref, sel_ref,
                        buf_ref, g_ref, sem_ref,
                        *, n_steps, chunks_per_seq, pages_per_chunk, n_heads, n_blocks, past_len):
    step = pl.program_id(0)
    slot = step % 2
    c = step % chunks_per_seq
    pages_per_block = MOBA_BLOCK // LANES
    blocks_per_chunk = pages_per_chunk // pages_per_block
    copies = functools.partial(_page_copies, pt_ref, kc_ref, buf_ref, sem_ref,
                               chunks_per_seq=chunks_per_seq, pages_per_chunk=pages_per_chunk)

    @pl.when(step == 0)
    def _():
        for cp in copies(step, slot):
            cp.start()

    @pl.when(step + 1 < n_steps)
    def _():
        for cp in copies(step + 1, 1 - slot):
            cp.start()

    for cp in copies(step, slot):
        cp.wait()

    @pl.when(c == 0)
    def _():
        g_ref[:, n_blocks:, :] = jnp.zeros((n_heads, g_ref.shape[1] - n_blocks, LANES), F32)

    qcol = qcol_ref[0]
    for h in range(n_heads):
        qb = jnp.broadcast_to(qcol[h * HEAD_DIM:(h + 1) * HEAD_DIM, :], (HEAD_DIM, LANES))
        for bl in range(blocks_per_chunk):
            part = None
            for pg in range(pages_per_block):
                kt = buf_ref[slot, bl * pages_per_block + pg, h * HEAD_DIM:(h + 1) * HEAD_DIM, :]
                s = jnp.sum(qb * kt, axis=0, keepdims=True)
                part = s if part is None else part + s
            g_ref[h, pl.ds(c * blocks_per_chunk + bl, 1), :] = part

    @pl.when(c == chunks_per_seq - 1)
    def _():
        own = past_len // MOBA_BLOCK
        nrow = g_ref.shape[1]
        lane = lax.broadcasted_iota(jnp.int32, (nrow, LANES), 1)
        sub = lax.broadcasted_iota(jnp.int32, (nrow, LANES), 0)
        kcol = kcol_ref[0]
        gate = jnp.full((nrow, LANES), -jnp.inf, F32)
        for h in range(n_heads):
            tot = jnp.sum(g_ref[h], axis=1, keepdims=True)
            own_dot = jnp.sum(qcol[h * HEAD_DIM:(h + 1) * HEAD_DIM, :] * kcol[h * HEAD_DIM:(h + 1) * HEAD_DIM, :],
                              axis=0, keepdims=True)
            tot = jnp.where(sub[:, 0:1] == own, own_dot, tot) * (1.0 / MOBA_BLOCK)
            gate = jnp.where(lane == h, tot, gate)
        gate = jnp.where(sub < own, gate, -jnp.inf)
        rows = []
        for _ in range(MOBA_TOPK):
            m = jnp.max(gate, axis=0, keepdims=True)
            idx = jnp.min(jnp.where(gate == m, sub, nrow), axis=0, keepdims=True)
            rows.append(idx)
            gate = jnp.where(sub == idx, -jnp.inf, gate)
        rows.append(jnp.zeros((SUBLANES - MOBA_TOPK, LANES), jnp.int32))
        sel_ref[0] = jnp.concatenate(rows, axis=0)


def _sample_gate(page_table, kc, qcol, kcol, *, n_heads, past_len, pages_per_chunk):
    bsz, n_pages = page_table.shape
    chunks_per_seq = n_pages // pages_per_chunk
    n_steps = bsz * chunks_per_seq
    n_blocks = n_pages * LANES // MOBA_BLOCK
    rows = ((n_blocks + 1 + SUBLANES - 1) // SUBLANES) * SUBLANES
    d_attn = n_heads * HEAD_DIM
    kern = functools.partial(_sample_gate_kernel, n_steps=n_steps, chunks_per_seq=chunks_per_seq,
                             pages_per_chunk=pages_per_chunk, n_heads=n_heads, n_blocks=n_blocks,
                             past_len=past_len)
    col_spec = pl.BlockSpec((1, d_attn, 1), lambda s, pt: (s // chunks_per_seq, 0, 0))
    grid_spec = pltpu.PrefetchScalarGridSpec(
        num_scalar_prefetch=1,
        grid=(n_steps,),
        in_specs=[pl.BlockSpec(memory_space=pl.ANY), col_spec, col_spec],
        out_specs=pl.BlockSpec((1, SUBLANES, LANES), lambda s, pt: (s // chunks_per_seq, 0, 0)),
        scratch_shapes=[
            pltpu.VMEM((2, pages_per_chunk, d_attn, LANES), F32),
            pltpu.VMEM((n_heads, rows, LANES), F32),
            pltpu.SemaphoreType.DMA((2,)),
        ],
    )
    return pl.pallas_call(
        kern,
        grid_spec=grid_spec,
        out_shape=jax.ShapeDtypeStruct((bsz, SUBLANES, LANES), jnp.int32),
        compiler_params=pltpu.CompilerParams(
            dimension_semantics=("arbitrary",), vmem_limit_bytes=40 * MIB),
        name="sample_block_gate",
    )(page_table, kc, qcol, kcol)


def _sel_copies(sel_ref, pt_ref, kc_ref, vc_ref, kbuf_ref, vbuf_ref, sem_ref, b, slot, *, n_heads):
    pages_per_block = MOBA_BLOCK // LANES
    copies = []
    for h in range(n_heads):
        for r in range(MOBA_TOPK):
            blk = sel_ref[(b * n_heads + h) * MOBA_TOPK + r]
            for pg in range(pages_per_block):
                phys = pt_ref[b, blk * pages_per_block + pg]
                rows = pl.ds(h * HEAD_DIM, HEAD_DIM)
                t = r * pages_per_block + pg
                copies.append(pltpu.make_async_copy(kc_ref.at[phys, rows], kbuf_ref.at[slot, h, t],
                                                    sem_ref.at[0, slot]))
                copies.append(pltpu.make_async_copy(vc_ref.at[phys, rows], vbuf_ref.at[slot, h, t],
                                                    sem_ref.at[1, slot]))
    return copies


def _sample_attn_kernel(sel_ref, pt_ref, kc_ref, vc_ref, qcol_ref, kcol_ref, vcol_ref, o_ref,
                        kbuf_ref, vbuf_ref, sem_ref, *, n_seq, n_heads, past_len):
    b = pl.program_id(0)
    slot = b % 2
    pages_per_block = MOBA_BLOCK // LANES
    n_tiles = MOBA_TOPK * pages_per_block
    copies = functools.partial(_sel_copies, sel_ref, pt_ref, kc_ref, vc_ref, kbuf_ref, vbuf_ref, sem_ref,
                               n_heads=n_heads)

    @pl.when(b == 0)
    def _():
        for cp in copies(b, slot):
            cp.start()

    @pl.when(b + 1 < n_seq)
    def _():
        for cp in copies(b + 1, 1 - slot):
            cp.start()

    for cp in copies(b, slot):
        cp.wait()

    qcol = qcol_ref[0]
    kcol = kcol_ref[0]
    vcol = vcol_ref[0]
    lane = lax.broadcasted_iota(jnp.int32, (1, LANES), 1)
    for h in range(n_heads):
        slope = 2.0 ** -(h + 1)
        hs = slice(h * HEAD_DIM, (h + 1) * HEAD_DIM)
        qb = jnp.broadcast_to(qcol[hs, :], (HEAD_DIM, LANES))
        s_own = jnp.sum(qcol[hs, :] * kcol[hs, :], axis=0, keepdims=True)
        s_tiles = []
        for t in range(n_tiles):
            blk = sel_ref[(b * n_heads + h) * MOBA_TOPK + t // pages_per_block]
            pos0 = blk * MOBA_BLOCK + (t % pages_per_block) * LANES
            dist = (past_len - pos0 - lane).astype(F32)
            s = jnp.sum(qb * kbuf_ref[slot, h, t], axis=0, keepdims=True) - slope * dist
            s_tiles.append(s)
        m = s_own
        for s in s_tiles:
            m = jnp.maximum(m, jnp.max(s, axis=1, keepdims=True))
        p_own = jnp.exp(s_own - m)
        l = p_own
        acc = p_own * vcol[hs, :]
        for t, s in enumerate(s_tiles):
            pr = jnp.exp(s - m)
            l = l + jnp.sum(pr, axis=1, keepdims=True)
            acc = acc + jnp.sum(vbuf_ref[slot, h, t] * pr, axis=1, keepdims=True)
        o_ref[0, hs, :] = acc * (1.0 / l)


def _sample_attention(sel_flat, page_table, kc, vc, qcol, kcol, vcol, *, n_heads, past_len):
    bsz = page_table.shape[0]
    d_attn = n_heads * HEAD_DIM
    n_tiles = MOBA_TOPK * (MOBA_BLOCK // LANES)
    kern = functools.partial(_sample_attn_kernel, n_seq=bsz, n_heads=n_heads, past_len=past_len)
    col_spec = pl.BlockSpec((1, d_attn, 1), lambda b, sel, pt: (b, 0, 0))
    any_spec = pl.BlockSpec(memory_space=pl.ANY)
    grid_spec = pltpu.PrefetchScalarGridSpec(
        num_scalar_prefetch=2,
        grid=(bsz,),
        in_specs=[any_spec, any_spec, col_spec, col_spec, col_spec],
        out_specs=col_spec,
        scratch_shapes=[
            pltpu.VMEM((2, n_heads, n_tiles, HEAD_DIM, LANES), F32),
            pltpu.VMEM((2, n_heads, n_tiles, HEAD_DIM, LANES), F32),
            pltpu.SemaphoreType.DMA((2, 2)),
        ],
    )
    return pl.pallas_call(
        kern,
        grid_spec=grid_spec,
        out_shape=jax.ShapeDtypeStruct((bsz, d_attn, 1), F32),
        compiler_params=pltpu.CompilerParams(
            dimension_semantics=("arbitrary",), vmem_limit_bytes=32 * MIB),
        name="sample_moba_attention",
    )(sel_flat, page_table, kc, vc, qcol, kcol, vcol)


def _sample_ffn_kernel(x_ref, att_ref, cm_ref, wo_ref, g1_ref, b1_ref, wup_ref, wfc_ref, sf0_ref, sf1_ref,
                       wdn_ref, g2_ref, b2_ref, y_ref, up_ref, *, d_attn, d_ff, alpha):
    mix = (_dot(att_ref[...].astype(BF16), wo_ref[0:d_attn, :])
           + _dot(cm_ref[...].astype(BF16), wo_ref[d_attn:, :]))
    x1 = _layer_norm(alpha * x_ref[...] + mix, g1_ref[...], b1_ref[...])
    up = _dot(x1.astype(BF16), wup_ref[...])
    up_ref[...] = up
    w = wfc_ref[...]
    conv = w[0:1, :] * sf0_ref[...] + w[1:2, :] * sf1_ref[...] + w[2:3, :] * up
    hcol = (_silu(conv[:, 0:d_ff]) * conv[:, d_ff:]).astype(BF16)
    f = _dot(hcol, wdn_ref[...])
    y_ref[...] = _layer_norm(alpha * x1 + f, g2_ref[...], b2_ref[...])


def _sample_ffn(xs, att, cm, wo, g1, b1, wup, wfc, sf0, sf1, wdn, g2, b2, *, alpha):
    b, d_model = xs.shape
    d_ff = wdn.shape[0]
    kern = functools.partial(_sample_ffn_kernel, d_attn=att.shape[1], d_ff=d_ff, alpha=alpha)
    return pl.pallas_call(
        kern,
        out_shape=[jax.ShapeDtypeStruct((b, d_model), F32), jax.ShapeDtypeStruct((b, 2 * d_ff), F32)],
        compiler_params=pltpu.CompilerParams(vmem_limit_bytes=48 * MIB),
        name="sample_outproj_ffn",
    )(xs, att, cm, wo, g1, b1, wup, wfc, sf0, sf1, wdn, g2, b2)


def kernel(x_prompt, x_sample, cache_k, cache_v, state_conv, state_ffn_conv, page_table, w_in, w_conv, w_out,
           ln1_g, ln1_b, w_ffn_up, w_ffn_conv, w_ffn_down, ln2_g, ln2_b):
    depth = w_in.shape[0]
    assert depth == 1, "single-layer step"
    bsz, seq, d_model = x_prompt.shape
    dec_b, dec_t, _ = x_sample.shape
    assert dec_t == 1
    _, n_phys, page_size, n_heads, head_dim = cache_k.shape
    assert head_dim == HEAD_DIM and page_size == LANES
    d_attn = n_heads * head_dim
    d_conv = (w_in.shape[2] - 3 * d_attn) // 3
    d_ff = w_ffn_down.shape[1]
    past_len = page_table.shape[1] * page_size
    alpha = (2.0 * depth) ** 0.25
    row = lambda a: a.reshape(1, -1)

    w_in_b = w_in[0].astype(BF16)
    wt = w_in_b[:, :3 * d_attn].T
    wr = w_in_b[:, d_attn:2 * d_attn]
    wr = jnp.concatenate([wr, w_in_b[:, 3 * d_attn:]], axis=1)
    wo = w_out[0].astype(BF16)
    wup = w_ffn_up[0].astype(BF16)
    wdn = w_ffn_down[0].astype(BF16)
    wc, wfc = w_conv[0], w_ffn_conv[0]
    g1, b1, g2, b2 = row(ln1_g[0]), row(ln1_b[0]), row(ln2_g[0]), row(ln2_b[0])

    x2d = x_prompt.reshape(bsz * seq, d_model)
    qt, kt, vt, krm, kmean, cm, conv_p = _prompt_inproj(x2d, wt, wr, wc, bsz=bsz, seq=seq, tm=512)
    att = _prompt_attention(qt, krm, vt, kmean, bsz=bsz, seq=seq)
    y_p, ffn_p = _prompt_ffn(x2d, att, cm, wo, g1, b1, wup, wfc, wdn, g2, b2,
                             bsz=bsz, seq=seq, tm=512, cw=256, alpha=alpha)
    to_cache = lambda a: jnp.transpose(a.reshape(1, bsz, n_heads, head_dim, seq), (0, 1, 4, 2, 3))
    k_prompt, v_prompt = to_cache(kt), to_cache(vt)

    xs = x_sample.reshape(dec_b, d_model)
    st0, st1 = state_conv[0, :, 0, :], state_conv[0, :, 1, :]
    q_s, k_s, v_s, cm_s, u_s = _sample_inproj(xs, w_in_b, wc, st0, st1, d_attn=d_attn, d_conv=d_conv)
    kc = jnp.transpose(cache_k[0], (0, 2, 3, 1)).reshape(n_phys, d_attn, page_size)
    vc = jnp.transpose(cache_v[0], (0, 2, 3, 1)).reshape(n_phys, d_attn, page_size)
    qcol, kcol, vcol = q_s[:, :, None], k_s[:, :, None], v_s[:, :, None]
    sel = _sample_gate(page_table, kc, qcol, kcol, n_heads=n_heads, past_len=past_len, pages_per_chunk=32)
    sel_flat = jnp.transpose(sel[:, :MOBA_TOPK, :n_heads], (0, 2, 1)).reshape(-1)
    att_s = _sample_attention(sel_flat, page_table, kc, vc, qcol, kcol, vcol,
                              n_heads=n_heads, past_len=past_len)
    sf0, sf1 = state_ffn_conv[0, :, 0, :], state_ffn_conv[0, :, 1, :]
    y_s, up_s = _sample_ffn(xs, att_s.reshape(dec_b, d_attn), cm_s, wo, g1, b1, wup, wfc, sf0, sf1,
                            wdn, g2, b2, alpha=alpha)

    return (y_p.reshape(bsz, seq, d_model),
            y_s.reshape(dec_b, 1, d_model),
            k_prompt, v_prompt,
            conv_p[None], ffn_p[None],
            k_s.reshape(1, dec_b, 1, n_heads, head_dim), v_s.reshape(1, dec_b, 1, n_heads, head_dim),
            jnp.stack([st1, u_s], axis=1)[None],
            jnp.stack([sf1, up_s], axis=1)[None])
```

```python
import functools

import jax
import jax.numpy as jnp
from jax import lax
from jax.experimental import pallas as pl
from jax.experimental.pallas import tpu as pltpu

F32 = jnp.float32
BF16 = jnp.bfloat16

HEAD_DIM = 64
MOBA_BLOCK = 256
MOBA_TOPK = 3
CONV_W = 3
LN_EPS = 1e-5
LOG2E = 1.4426950408889634
NEG = -1e30
SUBLANES = 8
LANES = 128
MIB = 1024 * 1024

_NT = (((1,), (1,)), ((), ()))


def _dot(a, b):
    return jnp.dot(a, b, preferred_element_type=F32)


def _dot_nt(a, b):
    return lax.dot_general(a, b, _NT, preferred_element_type=F32)


def _layer_norm(x, g, b):
    mu = jnp.mean(x, axis=-1, keepdims=True)
    xc = x - mu
    var = jnp.mean(xc * xc, axis=-1, keepdims=True)
    return xc * lax.rsqrt(var + LN_EPS) * g + b


def _silu(a):
    return a * jax.nn.sigmoid(a)


def _inproj_kernel(x_ref, wt_ref, wr_ref, wc_ref,
                   qt_ref, kt_ref, vt_ref, krm_ref, kmean_ref, cm_ref, cst_ref,
                   ubuf_ref, *, tm, tiles_per_batch, d_attn, d_conv):
    t = pl.program_id(0)
    tb = t % tiles_per_batch
    xb = x_ref[...].astype(BF16)

    qt_ref[0] = (_dot_nt(wt_ref[0:d_attn, :], xb) * (LOG2E / HEAD_DIM ** 0.5)).astype(BF16)
    kt_ref[0] = _dot_nt(wt_ref[d_attn:2 * d_attn, :], xb)
    vt_ref[0] = _dot_nt(wt_ref[2 * d_attn:3 * d_attn, :], xb)

    k = _dot(xb, wr_ref[:, 0:d_attn])
    krm_ref[...] = k.astype(BF16)
    for blk in range(tm // MOBA_BLOCK):
        ksum = jnp.sum(k[blk * MOBA_BLOCK:(blk + 1) * MOBA_BLOCK, :], axis=0, keepdims=True)
        kmean_ref[0, pl.ds(tb * (tm // MOBA_BLOCK) + blk, 1), :] = ksum * (1.0 / MOBA_BLOCK)

    gb = _dot(xb, wr_ref[:, d_attn:d_attn + d_conv])
    gc = _dot(xb, wr_ref[:, d_attn + d_conv:d_attn + 2 * d_conv])
    hh = _dot(xb, wr_ref[:, d_attn + 2 * d_conv:d_attn + 3 * d_conv])
    u = gc * hh

    @pl.when(tb == 0)
    def _():
        ubuf_ref[0:SUBLANES, :] = jnp.zeros((SUBLANES, d_conv), F32)

    ubuf_ref[SUBLANES:SUBLANES + tm, :] = u
    wc = wc_ref[...]
    conv = (wc[0:1, :] * ubuf_ref[SUBLANES - 2:SUBLANES - 2 + tm, :]
            + wc[1:2, :] * ubuf_ref[SUBLANES - 1:SUBLANES - 1 + tm, :]
            + wc[2:3, :] * u)
    cm_ref[...] = (gb * conv).astype(BF16)
    cst_ref[0] = ubuf_ref[SUBLANES + tm - (CONV_W - 1):SUBLANES + tm, :]
    ubuf_ref[0:SUBLANES, :] = ubuf_ref[tm:tm + SUBLANES, :]


def _prompt_inproj(x2d, wt, wr, w_conv, *, bsz, seq, tm):
    n, d_model = x2d.shape
    d_attn = wt.shape[0] // 3
    d_conv = (wr.shape[1] - d_attn) // 3
    tpb = seq // tm
    nb = seq // MOBA_BLOCK
    kern = functools.partial(_inproj_kernel, tm=tm, tiles_per_batch=tpb, d_attn=d_attn, d_conv=d_conv)
    const = lambda t: (0, 0)
    ct_spec = lambda: pl.BlockSpec((1, d_attn, tm), lambda t: (t // tpb, 0, t % tpb))
    return pl.pallas_call(
        kern,
        grid=(n // tm,),
        in_specs=[
            pl.BlockSpec((tm, d_model), lambda t: (t, 0)),
            pl.BlockSpec(wt.shape, const),
            pl.BlockSpec(wr.shape, const),
            pl.BlockSpec(w_conv.shape, const),
        ],
        out_specs=[
            ct_spec(), ct_spec(), ct_spec(),
            pl.BlockSpec((tm, d_attn), lambda t: (t, 0)),
            pl.BlockSpec((1, nb, d_attn), lambda t: (t // tpb, 0, 0)),
            pl.BlockSpec((tm, d_conv), lambda t: (t, 0)),
            pl.BlockSpec((1, CONV_W - 1, d_conv), lambda t: (t // tpb, 0, 0)),
        ],
        out_shape=[
            jax.ShapeDtypeStruct((bsz, d_attn, seq), BF16),
            jax.ShapeDtypeStruct((bsz, d_attn, seq), F32),
            jax.ShapeDtypeStruct((bsz, d_attn, seq), F32),
            jax.ShapeDtypeStruct((n, d_attn), BF16),
            jax.ShapeDtypeStruct((bsz, nb, d_attn), F32),
            jax.ShapeDtypeStruct((n, d_conv), BF16),
            jax.ShapeDtypeStruct((bsz, CONV_W - 1, d_conv), F32),
        ],
        scratch_shapes=[pltpu.VMEM((tm + 2 * SUBLANES, d_conv), F32)],
        compiler_params=pltpu.CompilerParams(
            dimension_semantics=("arbitrary",), vmem_limit_bytes=48 * MIB),
        name="prompt_inproj",
    )(x2d, wt, wr, w_conv)


BIAS_CH0 = 16


def _key_bias_channels(seq):
    pos = jnp.arange(seq, dtype=jnp.int32)[:, None]
    c = jnp.arange(2 * HEAD_DIM, dtype=jnp.int32)[None, :] - BIAS_CH0
    kblk, koff = pos // MOBA_BLOCK, pos % MOBA_BLOCK
    ext = jnp.where(c + BIAS_CH0 < SUBLANES, (kblk == c + BIAS_CH0).astype(F32), 0.0)
    ext = jnp.where((c >= 0) & (c < 3), koff.astype(F32), ext)
    ext = jnp.where((c >= 3) & (c < 6), (kblk * MOBA_BLOCK).astype(F32), ext)
    ext = jnp.where((c >= 6) & (c < 12), 1.0, ext)
    return ext.astype(BF16)


def _query_alibi_channels(coef, block, nq):
    row = lax.broadcasted_iota(jnp.int32, (2 * SUBLANES, nq), 0)
    qoff = lax.broadcasted_iota(jnp.int32, (2 * SUBLANES, nq), 1).astype(F32)
    cb = jnp.full((2 * SUBLANES, nq), 1.0, F32) * coef
    val = jnp.where(row < 6, cb,
                    jnp.where(row < 9, -cb * qoff,
                              jnp.where(row < 12, -cb * float(block * MOBA_BLOCK), 0.0)))
    hi = val.astype(BF16).astype(F32)
    mid = (val - hi).astype(BF16).astype(F32)
    lo = ((val - hi) - mid).astype(BF16).astype(F32)
    part = row % 3
    return jnp.where(part == 0, hi, jnp.where(part == 1, mid, lo))


def _selection_bias(g, n_past):
    sub = lax.broadcasted_iota(jnp.int32, g.shape, 0)
    rank = jnp.zeros(g.shape, jnp.int32)
    for jp in range(n_past):
        row = g[jp:jp + 1, :]
        beats = (row > g) | ((row == g) & (jp < sub))
        rank = rank + jnp.where(beats, 1, 0)
    sel = ((sub < n_past) & (rank < MOBA_TOPK)) | (sub == n_past)
    return jnp.where(sel, 0.0, NEG).astype(F32)


def _attn_kernel(qt_ref, k_ref, kx_ref, vt_ref, kmean_ref, o_ref, kk_ref, vtb_ref, *, nb):
    pair = pl.program_id(1)
    blk = MOBA_BLOCK
    pw = 2 * HEAD_DIM

    kk_ref[:, 0:pw] = k_ref[...]
    kk_ref[:, pw:2 * pw] = kx_ref[...]
    vtb_ref[...] = vt_ref[0].astype(BF16)

    slope_even = jnp.where(pair == 0, 2.0 ** -1,
                           jnp.where(pair == 1, 2.0 ** -3,
                                     jnp.where(pair == 2, 2.0 ** -5, 2.0 ** -7))).astype(F32)
    causal = (lax.broadcasted_iota(jnp.int32, (blk, blk), 1)
              >= lax.broadcasted_iota(jnp.int32, (blk, blk), 0))
    zeros8 = jnp.zeros((SUBLANES, blk), F32)
    zeros_h = jnp.zeros((HEAD_DIM, blk), BF16)
    zeros_tail = jnp.zeros((pw - 2 * BIAS_CH0, blk), BF16)
    kmean = kmean_ref[0]

    def scores(i, e):
        sl = slope_even * (0.5 ** e)
        qh = qt_ref[0, e * HEAD_DIM:(e + 1) * HEAD_DIM, i * blk:(i + 1) * blk]
        if i > 0:
            gate = jnp.dot(kmean[:, e * HEAD_DIM:(e + 1) * HEAD_DIM], qh.astype(F32),
                           preferred_element_type=F32, precision=lax.Precision.HIGHEST)
        else:
            gate = zeros8
        selb = _selection_bias(gate, i)
        q_ext = jnp.concatenate([qh, zeros_h] if e == 0 else [zeros_h, qh], axis=0)
        q_ext = jnp.concatenate(
            [q_ext,
             jnp.concatenate([selb, zeros8], axis=0).astype(BF16),
             _query_alibi_channels(sl * LOG2E, i, blk).astype(BF16),
             zeros_tail], axis=0)
        s = _dot(kk_ref[0:(i + 1) * blk, :], q_ext)
        s_own = jnp.where(causal, s[i * blk:, :], NEG)
        return s_own if i == 0 else jnp.concatenate([s[0:i * blk, :], s_own], axis=0)

    def softmax(s):
        m = jnp.max(s, axis=0, keepdims=True)
        pr = jnp.exp2(s - m)
        return pr.astype(BF16), jnp.sum(pr, axis=0, keepdims=True)

    def weighted_values(i, e, pr, l):
        acc = _dot(vtb_ref[e * HEAD_DIM:(e + 1) * HEAD_DIM, 0:(i + 1) * blk], pr)
        return acc * (1.0 / l)

    order = [(i, e) for i in range(nb) for e in range(2)]
    n_steps = len(order)
    s_buf, p_buf, outs = {}, {}, []
    for n in range(-1, n_steps + 1):
        if 0 <= n + 1 < n_steps:
            s_buf[n + 1] = scores(*order[n + 1])
        if 0 <= n < n_steps:
            p_buf[n] = softmax(s_buf.pop(n))
        if n >= 1:
            i, e = order[n - 1]
            outs.append(weighted_values(i, e, *p_buf.pop(n - 1)))
            if e == 1:
                o_ref[i * blk:(i + 1) * blk, :] = jnp.concatenate(outs, axis=0).T.astype(BF16)
                outs = []


def _prompt_attention(qt, krm, vt, kmean, *, bsz, seq):
    d_attn = qt.shape[1]
    nb = seq // MOBA_BLOCK
    assert nb <= SUBLANES, "selection-bias channels hold one block per row of a vector register"
    n_pairs = d_attn // (2 * HEAD_DIM)
    pw = 2 * HEAD_DIM
    kern = functools.partial(_attn_kernel, nb=nb)
    return pl.pallas_call(
        kern,
        grid=(bsz, n_pairs),
        in_specs=[
            pl.BlockSpec((1, pw, seq), lambda b, p: (b, p, 0)),
            pl.BlockSpec((seq, pw), lambda b, p: (b, p)),
            pl.BlockSpec((seq, pw), lambda b, p: (0, 0)),
            pl.BlockSpec((1, pw, seq), lambda b, p: (b, p, 0)),
            pl.BlockSpec((1, nb, pw), lambda b, p: (b, 0, p)),
        ],
        out_specs=pl.BlockSpec((seq, pw), lambda b, p: (b, p)),
        out_shape=jax.ShapeDtypeStruct((bsz * seq, d_attn), BF16),
        scratch_shapes=[
            pltpu.VMEM((seq, 2 * pw), BF16),
            pltpu.VMEM((pw, seq), BF16),
        ],
        compiler_params=pltpu.CompilerParams(
            dimension_semantics=("arbitrary", "arbitrary"), vmem_limit_bytes=48 * MIB),
        name="prompt_moba_attention",
    )(qt, krm, _key_bias_channels(seq), vt, kmean)


def _ffn_kernel(x_ref, att_ref, cm_ref, wo_ref, g1_ref, b1_ref, wup_ref, wfc_ref, wdn_ref, g2_ref, b2_ref,
                y_ref, fst_ref,
                x1_ref, x1b_ref, acc_ref, ua_ref, ug_ref, carry_ref,
                *, tm, tiles_per_batch, d_attn, d_ff, cw, alpha):
    t = pl.program_id(0)
    mix = _dot(att_ref[...], wo_ref[0:d_attn, :]) + _dot(cm_ref[...], wo_ref[d_attn:, :])
    x1 = _layer_norm(alpha * x_ref[...] + mix, g1_ref[...], b1_ref[...])
    x1_ref[...] = x1
    x1b_ref[...] = x1.astype(BF16)

    @pl.when(t % tiles_per_batch == 0)
    def _():
        carry_ref[...] = jnp.zeros(carry_ref.shape, F32)

    def up_project(col0, ubuf):
        ubuf[0:SUBLANES, :] = carry_ref[:, col0:col0 + cw]
        ubuf[SUBLANES:SUBLANES + tm, :] = _dot(x1b_ref[...], wup_ref[:, col0:col0 + cw])

    def token_conv(col0, ubuf):
        w = wfc_ref[:, col0:col0 + cw]
        conv = (w[0:1, :] * ubuf[SUBLANES - 2:SUBLANES - 2 + tm, :]
                + w[1:2, :] * ubuf[SUBLANES - 1:SUBLANES - 1 + tm, :]
                + w[2:3, :] * ubuf[SUBLANES:SUBLANES + tm, :])
        carry_ref[:, col0:col0 + cw] = ubuf[tm:tm + SUBLANES, :]
        fst_ref[0, :, col0:col0 + cw] = ubuf[SUBLANES + tm - (CONV_W - 1):SUBLANES + tm, :]
        return conv

    n_chunks = d_ff // cw
    hcol = {}
    for c in range(-1, n_chunks + 2):
        if 0 <= c + 1 < n_chunks:
            up_project((c + 1) * cw, ua_ref.at[(c + 1) % 2])
            up_project(d_ff + (c + 1) * cw, ug_ref.at[(c + 1) % 2])
        if 0 <= c < n_chunks:
            a = token_conv(c * cw, ua_ref.at[c % 2])
            g = token_conv(d_ff + c * cw, ug_ref.at[c % 2])
            hcol[c] = (_silu(a) * g).astype(BF16)
        if c >= 2:
            part = _dot(hcol.pop(c - 2), wdn_ref[(c - 2) * cw:(c - 1) * cw, :])
            if c == 2:
                acc_ref[...] = part
            else:
                acc_ref[...] += part

    y_ref[...] = _layer_norm(alpha * x1_ref[...] + acc_ref[...], g2_ref[...], b2_ref[...])


def _prompt_ffn(x2d, att, cm, wo, g1, b1, wup, wfc, wdn, g2, b2, *, bsz, seq, tm, cw, alpha):
    n, d_model = x2d.shape
    d_attn = att.shape[1]
    d_ff = wdn.shape[0]
    tpb = seq // tm
    kern = functools.partial(_ffn_kernel, tm=tm, tiles_per_batch=tpb, d_attn=d_attn, d_ff=d_ff, cw=cw,
                             alpha=alpha)
    const = lambda t: (0, 0)
    resident = lambda a: pl.BlockSpec(a.shape, const, pipeline_mode=pl.Buffered(1))
    row = lambda w: pl.BlockSpec((tm, w), lambda t: (t, 0))
    return pl.pallas_call(
        kern,
        grid=(n // tm,),
        in_specs=[row(d_model), row(d_attn), row(cm.shape[1]),
                  resident(wo), resident(g1), resident(b1), resident(wup), resident(wfc), resident(wdn),
                  resident(g2), resident(b2)],
        out_specs=[row(d_model),
                   pl.BlockSpec((1, CONV_W - 1, 2 * d_ff), lambda t: (t // tpb, 0, 0))],
        out_shape=[jax.ShapeDtypeStruct((n, d_model), F32),
                   jax.ShapeDtypeStruct((bsz, CONV_W - 1, 2 * d_ff), F32)],
        scratch_shapes=[
            pltpu.VMEM((tm, d_model), F32),
            pltpu.VMEM((tm, d_model), BF16),
            pltpu.VMEM((tm, d_model), F32),
            pltpu.VMEM((2, tm + 2 * SUBLANES, cw), F32),
            pltpu.VMEM((2, tm + 2 * SUBLANES, cw), F32),
            pltpu.VMEM((SUBLANES, 2 * d_ff), F32),
        ],
        compiler_params=pltpu.CompilerParams(
            dimension_semantics=("arbitrary",), vmem_limit_bytes=56 * MIB),
        name="prompt_outproj_ffn",
    )(x2d, att, cm, wo, g1, b1, wup, wfc, wdn, g2, b2)


def _sample_inproj_kernel(x_ref, w_ref, wc_ref, st0_ref, st1_ref,
                          q_ref, k_ref, v_ref, cm_ref, u_ref, *, d_attn, d_conv):
    xb = x_ref[...].astype(BF16)
    z = _dot(xb, w_ref[...])
    q_ref[...] = z[:, 0:d_attn] * 0.125
    k_ref[...] = z[:, d_attn:2 * d_attn]
    v_ref[...] = z[:, 2 * d_attn:3 * d_attn]
    o = 3 * d_attn
    gb = z[:, o:o + d_conv]
    u = z[:, o + d_conv:o + 2 * d_conv] * z[:, o + 2 * d_conv:o + 3 * d_conv]
    wc = wc_ref[...]
    conv = wc[0:1, :] * st0_ref[...] + wc[1:2, :] * st1_ref[...] + wc[2:3, :] * u
    cm_ref[...] = gb * conv
    u_ref[...] = u


def _sample_inproj(xs, w_in_b, w_conv, st0, st1, *, d_attn, d_conv):
    b = xs.shape[0]
    kern = functools.partial(_sample_inproj_kernel, d_attn=d_attn, d_conv=d_conv)
    sd = lambda w: jax.ShapeDtypeStruct((b, w), F32)
    return pl.pallas_call(
        kern,
        out_shape=[sd(d_attn), sd(d_attn), sd(d_attn), sd(d_conv), sd(d_conv)],
        compiler_params=pltpu.CompilerParams(vmem_limit_bytes=32 * MIB),
        name="sample_inproj",
    )(xs, w_in_b, w_conv, st0, st1)


def _page_copies(pt_ref, kc_ref, buf_ref, sem_ref, step, slot, *, chunks_per_seq, pages_per_chunk):
    b = step // chunks_per_seq
    c = step % chunks_per_seq
    copies = []
    for pg in range(pages_per_chunk):
        phys = pt_ref[b, c * pages_per_chunk + pg]
        copies.append(pltpu.make_async_copy(kc_ref.at[phys], buf_ref.at[slot, pg], sem_ref.at[slot]))
    return copies


def _sample_gate_kernel(pt_ref, kc_ref, qcol_ref, kcol_ref, sel_ref,
                        buf_ref, g_ref, sem_ref,
                        *, n_steps, chunks_per_seq, pages_per_chunk, n_heads, n_blocks, past_len):
    step = pl.program_id(0)
    slot = step % 2
    c = step % chunks_per_seq
    pages_per_block = MOBA_BLOCK // LANES
    blocks_per_chunk = pages_per_chunk // pages_per_block
    copies = functools.partial(_page_copies, pt_ref, kc_ref, buf_ref, sem_ref,
                               chunks_per_seq=chunks_per_seq, pages_per_chunk=pages_per_chunk)

    @pl.when(step == 0)
    def _():
        for cp in copies(step, slot):
            cp.start()

    @pl.when(step + 1 < n_steps)
    def _():
        for cp in copies(step + 1, 1 - slot):
            cp.start()

    for cp in copies(step, slot):
        cp.wait()

    @pl.when(c == 0)
    def _():
        g_ref[:, n_blocks:, :] = jnp.zeros((n_heads, g_ref.shape[1] - n_blocks, LANES), F32)

    qcol = qcol_ref[0]
    for h in range(n_heads):
        qb = jnp.broadcast_to(qcol[h * HEAD_DIM:(h + 1) * HEAD_DIM, :], (HEAD_DIM, LANES))
        for bl in range(blocks_per_chunk):
            part = None
            for pg in range(pages_per_block):
                kt = buf_ref[slot, bl * pages_per_block + pg, h * HEAD_DIM:(h + 1) * HEAD_DIM, :]
                s = jnp.sum(qb * kt, axis=0, keepdims=True)
                part = s if part is None else part + s
            g_ref[h, pl.ds(c * blocks_per_chunk + bl, 1), :] = part

    @pl.when(c == chunks_per_seq - 1)
    def _():
        own = past_len // MOBA_BLOCK
        nrow = g_ref.shape[1]
        lane = lax.broadcasted_iota(jnp.int32, (nrow, LANES), 1)
        sub = lax.broadcasted_iota(jnp.int32, (nrow, LANES), 0)
        kcol = kcol_ref[0]
        gate = jnp.full((nrow, LANES), -jnp.inf, F32)
        for h in range(n_heads):
            tot = jnp.sum(g_ref[h], axis=1, keepdims=True)
            own_dot = jnp.sum(qcol[h * HEAD_DIM:(h + 1) * HEAD_DIM, :] * kcol[h * HEAD_DIM:(h + 1) * HEAD_DIM, :],
                              axis=0, keepdims=True)
            tot = jnp.where(sub[:, 0:1] == own, own_dot, tot) * (1.0 / MOBA_BLOCK)
            gate = jnp.where(lane == h, tot, gate)
        gate = jnp.where(sub < own, gate, -jnp.inf)
        rows = []
        for _ in range(MOBA_TOPK):
            m = jnp.max(gate, axis=0, keepdims=True)
            idx = jnp.min(jnp.where(gate == m, sub, nrow), axis=0, keepdims=True)
            rows.append(idx)
            gate = jnp.where(sub == idx, -jnp.inf, gate)
        rows.append(jnp.zeros((SUBLANES - MOBA_TOPK, LANES), jnp.int32))
        sel_ref[0] = jnp.concatenate(rows, axis=0)


def _sample_gate(page_table, kc, qcol, kcol, *, n_heads, past_len, pages_per_chunk):
    bsz, n_pages = page_table.shape
    chunks_per_seq = n_pages // pages_per_chunk
    n_steps = bsz * chunks_per_seq
    n_blocks = n_pages * LANES // MOBA_BLOCK
    rows = ((n_blocks + 1 + SUBLANES - 1) // SUBLANES) * SUBLANES
    d_attn = n_heads * HEAD_DIM
    kern = functools.partial(_sample_gate_kernel, n_steps=n_steps, chunks_per_seq=chunks_per_seq,
                             pages_per_chunk=pages_per_chunk, n_heads=n_heads, n_blocks=n_blocks,
                             past_len=past_len)
    col_spec = pl.BlockSpec((1, d_attn, 1), lambda s, pt: (s // chunks_per_seq, 0, 0))
    grid_spec = pltpu.PrefetchScalarGridSpec(
        num_scalar_prefetch=1,
        grid=(n_steps,),
        in_specs=[pl.BlockSpec(memory_space=pl.ANY), col_spec, col_spec],
        out_specs=pl.BlockSpec((1, SUBLANES, LANES), lambda s, pt: (s // chunks_per_seq, 0, 0)),
        scratch_shapes=[
            pltpu.VMEM((2, pages_per_chunk, d_attn, LANES), F32),
            pltpu.VMEM((n_heads, rows, LANES), F32),
            pltpu.SemaphoreType.DMA((2,)),
        ],
    )
    return pl.pallas_call(
        kern,
        grid_spec=grid_spec,
        out_shape=jax.ShapeDtypeStruct((bsz, SUBLANES, LANES), jnp.int32),
        compiler_params=pltpu.CompilerParams(
            dimension_semantics=("arbitrary",), vmem_limit_bytes=40 * MIB),
        name="sample_block_gate",
    )(page_table, kc, qcol, kcol)


def _sel_copies(sel_ref, pt_ref, kc_ref, vc_ref, kbuf_ref, vbuf_ref, sem_ref, b, slot, *, n_heads):
    pages_per_block = MOBA_BLOCK // LANES
    copies = []
    for h in range(n_heads):
        for r in range(MOBA_TOPK):
            blk = sel_ref[(b * n_heads + h) * MOBA_TOPK + r]
            for pg in range(pages_per_block):
                phys = pt_ref[b, blk * pages_per_block + pg]
                rows = pl.ds(h * HEAD_DIM, HEAD_DIM)
                t = r * pages_per_block + pg
                copies.append(pltpu.make_async_copy(kc_ref.at[phys, rows], kbuf_ref.at[slot, h, t],
                                                    sem_ref.at[0, slot]))
                copies.append(pltpu.make_async_copy(vc_ref.at[phys, rows], vbuf_ref.at[slot, h, t],
                                                    sem_ref.at[1, slot]))
    return copies


def _sample_attn_kernel(sel_ref, pt_ref, kc_ref, vc_ref, qcol_ref, kcol_ref, vcol_ref, o_ref,
                        kbuf_ref, vbuf_ref, sem_ref, *, n_seq, n_heads, past_len):
    b = pl.program_id(0)
    slot = b % 2
    pages_per_block = MOBA_BLOCK // LANES
    n_tiles = MOBA_TOPK * pages_per_block
    copies = functools.partial(_sel_copies, sel_ref, pt_ref, kc_ref, vc_ref, kbuf_ref, vbuf_ref, sem_ref,
                               n_heads=n_heads)

    @pl.when(b == 0)
    def _():
        for cp in copies(b, slot):
            cp.start()

    @pl.when(b + 1 < n_seq)
    def _():
        for cp in copies(b + 1, 1 - slot):
            cp.start()

    for cp in copies(b, slot):
        cp.wait()

    qcol = qcol_ref[0]
    kcol = kcol_ref[0]
    vcol = vcol_ref[0]
    lane = lax.broadcasted_iota(jnp.int32, (1, LANES), 1)
    heads = range(n_heads)
    hs = [slice(h * HEAD_DIM, (h + 1) * HEAD_DIM) for h in heads]

    s_own, s_all = [], []
    for h in heads:
        qb = jnp.broadcast_to(qcol[hs[h], :], (HEAD_DIM, LANES))
        s_own.append(jnp.sum(qcol[hs[h], :] * kcol[hs[h], :], axis=0, keepdims=True))
        rows = []
        for t in range(n_tiles):
            blk = sel_ref[(b * n_heads + h) * MOBA_TOPK + t // pages_per_block]
            pos0 = blk * MOBA_BLOCK + (t % pages_per_block) * LANES
            dist = (past_len - pos0 - lane).astype(F32)
            rows.append(jnp.sum(qb * kbuf_ref[slot, h, t], axis=0, keepdims=True) - 2.0 ** -(h + 1) * dist)
        s_all.append(jnp.concatenate(rows, axis=0))
    m = [jnp.maximum(s_own[h], jnp.max(s_all[h], axis=(0, 1), keepdims=True)) for h in heads]
    pr = [jnp.exp(s_all[h] - m[h]) for h in heads]
    p_own = [jnp.exp(s_own[h] - m[h]) for h in heads]
    l = [p_own[h] + jnp.sum(pr[h], axis=(0, 1), keepdims=True) for h in heads]
    for h in heads:
        acc = p_own[h] * vcol[hs[h], :]
        wv = None
        for t in range(n_tiles):
            part = vbuf_ref[slot, h, t] * pr[h][t:t + 1, :]
            wv = part if wv is None else wv + part
        acc = acc + jnp.sum(wv, axis=1, keepdims=True)
        o_ref[0, hs[h], :] = acc * (1.0 / l[h])


def _sample_attention(sel_flat, page_table, kc, vc, qcol, kcol, vcol, *, n_heads, past_len):
    bsz = page_table.shape[0]
    d_attn = n_heads * HEAD_DIM
    n_tiles = MOBA_TOPK * (MOBA_BLOCK // LANES)
    kern = functools.partial(_sample_attn_kernel, n_seq=bsz, n_heads=n_heads, past_len=past_len)
    col_spec = pl.BlockSpec((1, d_attn, 1), lambda b, sel, pt: (b, 0, 0))
    any_spec = pl.BlockSpec(memory_space=pl.ANY)
    grid_spec = pltpu.PrefetchScalarGridSpec(
        num_scalar_prefetch=2,
        grid=(bsz,),
        in_specs=[any_spec, any_spec, col_spec, col_spec, col_spec],
        out_specs=col_spec,
        scratch_shapes=[
            pltpu.VMEM((2, n_heads, n_tiles, HEAD_DIM, LANES), F32),
            pltpu.VMEM((2, n_heads, n_tiles, HEAD_DIM, LANES), F32),
            pltpu.SemaphoreType.DMA((2, 2)),
        ],
    )
    return pl.pallas_call(
        kern,
        grid_spec=grid_spec,
        out_shape=jax.ShapeDtypeStruct((bsz, d_attn, 1), F32),
        compiler_params=pltpu.CompilerParams(
            dimension_semantics=("arbitrary",), vmem_limit_bytes=32 * MIB),
        name="sample_moba_attention",
    )(sel_flat, page_table, kc, vc, qcol, kcol, vcol)


def _sample_ffn_kernel(x_ref, att_ref, cm_ref, wo_ref, g1_ref, b1_ref, wup_ref, wfc_ref, sf0_ref, sf1_ref,
                       wdn_ref, g2_ref, b2_ref, y_ref, up_ref, *, d_attn, d_ff, alpha):
    mix = (_dot(att_ref[...].astype(BF16), wo_ref[0:d_attn, :])
           + _dot(cm_ref[...].astype(BF16), wo_ref[d_attn:, :]))
    x1 = _layer_norm(alpha * x_ref[...] + mix, g1_ref[...], b1_ref[...])
    up = _dot(x1.astype(BF16), wup_ref[...])
    up_ref[...] = up
    w = wfc_ref[...]
    conv = w[0:1, :] * sf0_ref[...] + w[1:2, :] * sf1_ref[...] + w[2:3, :] * up
    hcol = (_silu(conv[:, 0:d_ff]) * conv[:, d_ff:]).astype(BF16)
    f = _dot(hcol, wdn_ref[...])
    y_ref[...] = _layer_norm(alpha * x1 + f, g2_ref[...], b2_ref[...])


def _sample_ffn(xs, att, cm, wo, g1, b1, wup, wfc, sf0, sf1, wdn, g2, b2, *, alpha):
    b, d_model = xs.shape
    d_ff = wdn.shape[0]
    kern = functools.partial(_sample_ffn_kernel, d_attn=att.shape[1], d_ff=d_ff, alpha=alpha)
    return pl.pallas_call(
        kern,
        out_shape=[jax.ShapeDtypeStruct((b, d_model), F32), jax.ShapeDtypeStruct((b, 2 * d_ff), F32)],
        compiler_params=pltpu.CompilerParams(vmem_limit_bytes=48 * MIB),
        name="sample_outproj_ffn",
    )(xs, att, cm, wo, g1, b1, wup, wfc, sf0, sf1, wdn, g2, b2)


def kernel(x_prompt, x_sample, cache_k, cache_v, state_conv, state_ffn_conv, page_table, w_in, w_conv, w_out,
           ln1_g, ln1_b, w_ffn_up, w_ffn_conv, w_ffn_down, ln2_g, ln2_b):
    depth = w_in.shape[0]
    assert depth == 1, "single-layer step"
    bsz, seq, d_model = x_prompt.shape
    dec_b, dec_t, _ = x_sample.shape
    assert dec_t == 1
    _, n_phys, page_size, n_heads, head_dim = cache_k.shape
    assert head_dim == HEAD_DIM and page_size == LANES
    d_attn = n_heads * head_dim
    d_conv = (w_in.shape[2] - 3 * d_attn) // 3
    d_ff = w_ffn_down.shape[1]
    past_len = page_table.shape[1] * page_size
    alpha = (2.0 * depth) ** 0.25
    row = lambda a: a.reshape(1, -1)

    w_in_b = w_in[0].astype(BF16)
    wt = w_in_b[:, :3 * d_attn].T
    wr = w_in_b[:, d_attn:2 * d_attn]
    wr = jnp.concatenate([wr, w_in_b[:, 3 * d_attn:]], axis=1)
    wo = w_out[0].astype(BF16)
    wup = w_ffn_up[0].astype(BF16)
    wdn = w_ffn_down[0].astype(BF16)
    wc, wfc = w_conv[0], w_ffn_conv[0]
    g1, b1, g2, b2 = row(ln1_g[0]), row(ln1_b[0]), row(ln2_g[0]), row(ln2_b[0])

    x2d = x_prompt.reshape(bsz * seq, d_model)
    qt, kt, vt, krm, kmean, cm, conv_p = _prompt_inproj(x2d, wt, wr, wc, bsz=bsz, seq=seq, tm=512)
    att = _prompt_attention(qt, krm, vt, kmean, bsz=bsz, seq=seq)
    y_p, ffn_p = _prompt_ffn(x2d, att, cm, wo, g1, b1, wup, wfc, wdn, g2, b2,
                             bsz=bsz, seq=seq, tm=512, cw=256, alpha=alpha)
    to_cache = lambda a: jnp.transpose(a.reshape(1, bsz, n_heads, head_dim, seq), (0, 1, 4, 2, 3))
    k_prompt, v_prompt = to_cache(kt), to_cache(vt)

    xs = x_sample.reshape(dec_b, d_model)
    st0, st1 = state_conv[0, :, 0, :], state_conv[0, :, 1, :]
    q_s, k_s, v_s, cm_s, u_s = _sample_inproj(xs, w_in_b, wc, st0, st1, d_attn=d_attn, d_conv=d_conv)
    kc = jnp.transpose(cache_k[0], (0, 2, 3, 1)).reshape(n_phys, d_attn, page_size)
    vc = jnp.transpose(cache_v[0], (0, 2, 3, 1)).reshape(n_phys, d_attn, page_size)
    qcol, kcol, vcol = q_s[:, :, None], k_s[:, :, None], v_s[:, :, None]
    sel = _sample_gate(page_table, kc, qcol, kcol, n_heads=n_heads, past_len=past_len, pages_per_chunk=32)
    sel_flat = jnp.transpose(sel[:, :MOBA_TOPK, :n_heads], (0, 2, 1)).reshape(-1)
    att_s = _sample_attention(sel_flat, page_table, kc, vc, qcol, kcol, vcol,
                              n_heads=n_heads, past_len=past_len)
    sf0, sf1 = state_ffn_conv[0, :, 0, :], state_ffn_conv[0, :, 1, :]
    y_s, up_s = _sample_ffn(xs, att_s.reshape(dec_b, d_attn), cm_s, wo, g1, b1, wup, wfc, sf0, sf1,
                            wdn, g2, b2, alpha=alpha)

    return (y_p.reshape(bsz, seq, d_model),
            y_s.reshape(dec_b, 1, d_model),
            k_prompt, v_prompt,
            conv_p[None], ffn_p[None],
            k_s.reshape(1, dec_b, 1, n_heads, head_dim), v_s.reshape(1, dec_b, 1, n_heads, head_dim),
            jnp.stack([st1, u_s], axis=1)[None],
            jnp.stack([sf1, up_s], axis=1)[None])
```

```python
import functools

import jax
import jax.numpy as jnp
from jax import lax
from jax.experimental import pallas as pl
from jax.experimental.pallas import tpu as pltpu

F32 = jnp.float32
BF16 = jnp.bfloat16

HEAD_DIM = 64
MOBA_BLOCK = 256
MOBA_TOPK = 3
CONV_W = 3
LN_EPS = 1e-5
LOG2E = 1.4426950408889634
NEG = -1e30
SUBLANES = 8
LANES = 128
MIB = 1024 * 1024

_NT = (((1,), (1,)), ((), ()))


def _dot(a, b):
    return jnp.dot(a, b, preferred_element_type=F32)


def _dot_nt(a, b):
    return lax.dot_general(a, b, _NT, preferred_element_type=F32)


def _layer_norm(x, g, b):
    mu = jnp.mean(x, axis=-1, keepdims=True)
    xc = x - mu
    var = jnp.mean(xc * xc, axis=-1, keepdims=True)
    return xc * lax.rsqrt(var + LN_EPS) * g + b


def _silu(a):
    return a * jax.nn.sigmoid(a)


def _inproj_kernel(x_ref, wt_ref, wr_ref, wc_ref,
                   qt_ref, kt_ref, vt_ref, krm_ref, kmean_ref, cm_ref, cst_ref,
                   ubuf_ref, *, tm, tiles_per_batch, d_attn, d_conv):
    t = pl.program_id(0)
    tb = t % tiles_per_batch
    @pl.when(tb == 0)
    def _():
        ubuf_ref[0:SUBLANES, :] = jnp.zeros((SUBLANES, d_conv), F32)

    xb = x_ref[...].astype(BF16)

    gc = _dot(xb, wr_ref[:, d_attn + d_conv:d_attn + 2 * d_conv])
    hh = _dot(xb, wr_ref[:, d_attn + 2 * d_conv:d_attn + 3 * d_conv])
    u = gc * hh
    gb = _dot(xb, wr_ref[:, d_attn:d_attn + d_conv])
    ubuf_ref[SUBLANES:SUBLANES + tm, :] = u
    wc = wc_ref[...]
    conv = (wc[0:1, :] * ubuf_ref[SUBLANES - 2:SUBLANES - 2 + tm, :]
            + wc[1:2, :] * ubuf_ref[SUBLANES - 1:SUBLANES - 1 + tm, :]
            + wc[2:3, :] * u)
    cm_ref[...] = (gb * conv).astype(BF16)
    cst_ref[0] = ubuf_ref[SUBLANES + tm - (CONV_W - 1):SUBLANES + tm, :]
    ubuf_ref[0:SUBLANES, :] = ubuf_ref[tm:tm + SUBLANES, :]

    k = _dot(xb, wr_ref[:, 0:d_attn])
    kt_ref[0] = k.T
    krm_ref[...] = k.astype(BF16)
    for blk in range(tm // MOBA_BLOCK):
        ksum = jnp.sum(k[blk * MOBA_BLOCK:(blk + 1) * MOBA_BLOCK, :], axis=0, keepdims=True)
        kmean_ref[0, pl.ds(tb * (tm // MOBA_BLOCK) + blk, 1), :] = ksum * (1.0 / MOBA_BLOCK)

    qt_ref[0] = (_dot_nt(wt_ref[0:d_attn, :], xb) * (LOG2E / HEAD_DIM ** 0.5)).astype(BF16)
    vt_ref[0] = _dot_nt(wt_ref[2 * d_attn:3 * d_attn, :], xb)


def _prompt_inproj(x2d, wt, wr, w_conv, *, bsz, seq, tm):
    n, d_model = x2d.shape
    d_attn = wt.shape[0] // 3
    d_conv = (wr.shape[1] - d_attn) // 3
    tpb = seq // tm
    nb = seq // MOBA_BLOCK
    kern = functools.partial(_inproj_kernel, tm=tm, tiles_per_batch=tpb, d_attn=d_attn, d_conv=d_conv)
    const = lambda t: (0, 0)
    ct_spec = lambda: pl.BlockSpec((1, d_attn, tm), lambda t: (t // tpb, 0, t % tpb))
    return pl.pallas_call(
        kern,
        grid=(n // tm,),
        in_specs=[
            pl.BlockSpec((tm, d_model), lambda t: (t, 0)),
            pl.BlockSpec(wt.shape, const),
            pl.BlockSpec(wr.shape, const),
            pl.BlockSpec(w_conv.shape, const),
        ],
        out_specs=[
            ct_spec(), ct_spec(), ct_spec(),
            pl.BlockSpec((tm, d_attn), lambda t: (t, 0)),
            pl.BlockSpec((1, nb, d_attn), lambda t: (t // tpb, 0, 0)),
            pl.BlockSpec((tm, d_conv), lambda t: (t, 0)),
            pl.BlockSpec((1, CONV_W - 1, d_conv), lambda t: (t // tpb, 0, 0)),
        ],
        out_shape=[
            jax.ShapeDtypeStruct((bsz, d_attn, seq), BF16),
            jax.ShapeDtypeStruct((bsz, d_attn, seq), F32),
            jax.ShapeDtypeStruct((bsz, d_attn, seq), F32),
            jax.ShapeDtypeStruct((n, d_attn), BF16),
            jax.ShapeDtypeStruct((bsz, nb, d_attn), F32),
            jax.ShapeDtypeStruct((n, d_conv), BF16),
            jax.ShapeDtypeStruct((bsz, CONV_W - 1, d_conv), F32),
        ],
        scratch_shapes=[pltpu.VMEM((tm + 2 * SUBLANES, d_conv), F32)],
        compiler_params=pltpu.CompilerParams(
            dimension_semantics=("arbitrary",), vmem_limit_bytes=48 * MIB),
        name="prompt_inproj",
    )(x2d, wt, wr, w_conv)


BIAS_CH0 = 16


def _key_bias_channels(seq):
    pos = jnp.arange(seq, dtype=jnp.int32)[:, None]
    c = jnp.arange(2 * HEAD_DIM, dtype=jnp.int32)[None, :] - BIAS_CH0
    kblk, koff = pos // MOBA_BLOCK, pos % MOBA_BLOCK
    ext = jnp.where(c + BIAS_CH0 < SUBLANES, (kblk == c + BIAS_CH0).astype(F32), 0.0)
    ext = jnp.where((c >= 0) & (c < 3), koff.astype(F32), ext)
    ext = jnp.where((c >= 3) & (c < 6), (kblk * MOBA_BLOCK).astype(F32), ext)
    ext = jnp.where((c >= 6) & (c < 12), 1.0, ext)
    return ext.astype(BF16)


def _query_alibi_channels(coef, block, nq):
    row = lax.broadcasted_iota(jnp.int32, (2 * SUBLANES, nq), 0)
    qoff = lax.broadcasted_iota(jnp.int32, (2 * SUBLANES, nq), 1).astype(F32)
    cb = jnp.full((2 * SUBLANES, nq), 1.0, F32) * coef
    val = jnp.where(row < 6, cb,
                    jnp.where(row < 9, -cb * qoff,
                              jnp.where(row < 12, -cb * float(block * MOBA_BLOCK), 0.0)))
    hi = val.astype(BF16).astype(F32)
    mid = (val - hi).astype(BF16).astype(F32)
    lo = ((val - hi) - mid).astype(BF16).astype(F32)
    part = row % 3
    return jnp.where(part == 0, hi, jnp.where(part == 1, mid, lo))


def _selection_bias(g, n_past):
    sub = lax.broadcasted_iota(jnp.int32, g.shape, 0)
    rank = jnp.zeros(g.shape, jnp.int32)
    for jp in range(n_past):
        row = g[jp:jp + 1, :]
        beats = (row > g) | ((row == g) & (jp < sub))
        rank = rank + jnp.where(beats, 1, 0)
    sel = ((sub < n_past) & (rank < MOBA_TOPK)) | (sub == n_past)
    return jnp.where(sel, 0.0, NEG).astype(F32)


def _gate_wave_copies(pt_ref, kc_ref, buf_ref, sem_ref, wave, slot, *, pages_per_wave, pages_per_seq):
    first = wave * pages_per_wave
    seq_idx = first // pages_per_seq
    pg0 = first % pages_per_seq
    return [pltpu.make_async_copy(kc_ref.at[pt_ref[seq_idx, pg0 + pg]], buf_ref.at[slot, pg], sem_ref.at[slot])
            for pg in range(pages_per_wave)]


def _sample_gate_wave(copies, wave, w, part, n_parts, n_waves, qcol_ref, buf_ref, g_ref,
                      *, pages_per_wave, pages_per_seq, n_heads):
    slot = w % 2
    pages_per_block = MOBA_BLOCK // LANES
    blocks_per_part = pages_per_wave // pages_per_block // n_parts
    n_blocks = pages_per_seq // pages_per_block
    row0 = (wave * pages_per_wave % pages_per_seq) // pages_per_block

    if part == 0:
        for cp in copies(jnp.minimum(wave + 1, n_waves - 1), 1 - slot):
            cp.start()
        for cp in copies(wave, slot):
            cp.wait()
        g_ref[:, n_blocks:, :] = jnp.zeros((n_heads, g_ref.shape[1] - n_blocks, LANES), F32)

    qcol = qcol_ref[0]
    for h in range(n_heads):
        qb = jnp.broadcast_to(qcol[h * HEAD_DIM:(h + 1) * HEAD_DIM, :], (HEAD_DIM, LANES))
        for bl in range(part * blocks_per_part, (part + 1) * blocks_per_part):
            prod = None
            for pg in range(pages_per_block):
                kt = buf_ref[slot, bl * pages_per_block + pg, h * HEAD_DIM:(h + 1) * HEAD_DIM, :]
                prod = qb * kt if prod is None else prod + qb * kt
            g_ref[h, pl.ds(row0 + bl, 1), :] = jnp.sum(prod, axis=0, keepdims=True)


def _sample_gate_topk(qcol_ref, kcol_ref, sel_ref, g_ref, *, n_heads, past_len):
    qcol = qcol_ref[0]
    kcol = kcol_ref[0]
    own = past_len // MOBA_BLOCK
    nrow = g_ref.shape[1]
    lane = lax.broadcasted_iota(jnp.int32, (nrow, LANES), 1)
    sub = lax.broadcasted_iota(jnp.int32, (nrow, LANES), 0)
    gate = jnp.full((nrow, LANES), -jnp.inf, F32)
    for h in range(n_heads):
        tot = jnp.sum(g_ref[h], axis=1, keepdims=True)
        own_dot = jnp.sum(qcol[h * HEAD_DIM:(h + 1) * HEAD_DIM, :] * kcol[h * HEAD_DIM:(h + 1) * HEAD_DIM, :],
                          axis=0, keepdims=True)
        tot = jnp.where(sub[:, 0:1] == own, own_dot, tot) * (1.0 / MOBA_BLOCK)
        gate = jnp.where(lane == h, tot, gate)
    gate = jnp.where(sub < own, gate, -jnp.inf)
    rows = []
    for _ in range(MOBA_TOPK):
        m = jnp.max(gate, axis=0, keepdims=True)
        idx = jnp.min(jnp.where(gate == m, sub, nrow), axis=0, keepdims=True)
        rows.append(idx)
        gate = jnp.where(sub == idx, -jnp.inf, gate)
    rows.append(jnp.zeros((SUBLANES - MOBA_TOPK, LANES), jnp.int32))
    sel_ref[0] = jnp.concatenate(rows, axis=0)


def _attn_kernel(pt_ref, qt_ref, k_ref, kx_ref, vt_ref, kmean_ref, kc_ref, qcol_ref, kcol_ref,
                 o_ref, sel_ref,
                 kk_ref, vtb_ref, pbuf_ref, g_ref, sem_ref,
                 *, nb, n_grid, waves_per_step, pages_per_wave, pages_per_seq, n_heads, past_len):
    pair = pl.program_id(1)
    step = pl.program_id(0) * pl.num_programs(1) + pair
    blk = MOBA_BLOCK
    pw = 2 * HEAD_DIM

    n_waves = n_grid * waves_per_step
    copies = functools.partial(_gate_wave_copies, pt_ref, kc_ref, pbuf_ref, sem_ref,
                               pages_per_wave=pages_per_wave, pages_per_seq=pages_per_seq)
    gate_wave = functools.partial(_sample_gate_wave, copies, qcol_ref=qcol_ref, buf_ref=pbuf_ref, g_ref=g_ref,
                                  n_waves=n_waves, pages_per_wave=pages_per_wave, pages_per_seq=pages_per_seq,
                                  n_heads=n_heads)

    @pl.when(step == 0)
    def _():
        for cp in copies(0, 0):
            cp.start()

    kk_ref[:, 0:pw] = k_ref[...]
    kk_ref[:, pw:2 * pw] = kx_ref[...]
    vtb_ref[...] = vt_ref[0].astype(BF16)

    slope_even = jnp.where(pair == 0, 2.0 ** -1,
                           jnp.where(pair == 1, 2.0 ** -3,
                                     jnp.where(pair == 2, 2.0 ** -5, 2.0 ** -7))).astype(F32)
    causal = (lax.broadcasted_iota(jnp.int32, (blk, blk), 1)
              >= lax.broadcasted_iota(jnp.int32, (blk, blk), 0))
    zeros8 = jnp.zeros((SUBLANES, blk), F32)
    zeros_h = jnp.zeros((HEAD_DIM, blk), BF16)
    zeros_tail = jnp.zeros((pw - 2 * BIAS_CH0, blk), BF16)
    kmean = kmean_ref[0]

    def scores(i, e):
        sl = slope_even * (0.5 ** e)
        qh = qt_ref[0, e * HEAD_DIM:(e + 1) * HEAD_DIM, i * blk:(i + 1) * blk]
        if i > 0:
            gate = jnp.dot(kmean[:, e * HEAD_DIM:(e + 1) * HEAD_DIM], qh.astype(F32),
                           preferred_element_type=F32, precision=lax.Precision.HIGHEST)
        else:
            gate = zeros8
        selb = _selection_bias(gate, i)
        q_ext = jnp.concatenate([qh, zeros_h] if e == 0 else [zeros_h, qh], axis=0)
        q_ext = jnp.concatenate(
            [q_ext,
             jnp.concatenate([selb, zeros8], axis=0).astype(BF16),
             _query_alibi_channels(sl * LOG2E, i, blk).astype(BF16),
             zeros_tail], axis=0)
        s = _dot(kk_ref[0:(i + 1) * blk, :], q_ext)
        s_own = jnp.where(causal, s[i * blk:, :], NEG)
        return s_own if i == 0 else jnp.concatenate([s[0:i * blk, :], s_own], axis=0)

    def softmax(s):
        m = jnp.max(s, axis=0, keepdims=True)
        pr = jnp.exp2(s - m)
        return pr.astype(BF16), jnp.sum(pr, axis=0, keepdims=True)

    def weighted_values(i, e, pr, l):
        acc = _dot(vtb_ref[e * HEAD_DIM:(e + 1) * HEAD_DIM, 0:(i + 1) * blk], pr)
        return acc * (1.0 / l)

    order = [(i, e) for i in range(nb) for e in range(2)]
    n_steps = len(order)
    s_buf, p_buf, outs = {}, {}, []
    parts_per_wave = n_steps // waves_per_step
    work = [i + 1 for i, _ in order]
    gate_parts_at = {n: [] for n in range(n_steps)}
    for k in range(n_steps):
        target = (k + 0.5) * sum(work) / n_steps
        gate_parts_at[next(n for n in range(n_steps) if sum(work[:n + 1]) >= target)].append(k)
    for n in range(-1, n_steps + 1):
        for k in gate_parts_at.get(n + 1, ()):
            w, part = divmod(k, parts_per_wave)
            gate_wave(step * waves_per_step + w, w, part, parts_per_wave)
        if 0 <= n + 1 < n_steps:
            s_buf[n + 1] = scores(*order[n + 1])
        if 0 <= n < n_steps:
            p_buf[n] = softmax(s_buf.pop(n))
        if n >= 1:
            i, e = order[n - 1]
            outs.append(weighted_values(i, e, *p_buf.pop(n - 1)))
            if e == 1:
                o_ref[i * blk:(i + 1) * blk, :] = jnp.concatenate(outs, axis=0).T.astype(BF16)
                outs = []

    @pl.when((step + 1) * waves_per_step * pages_per_wave % pages_per_seq == 0)
    def _():
        _sample_gate_topk(qcol_ref, kcol_ref, sel_ref, g_ref, n_heads=n_heads, past_len=past_len)

    @pl.when(step == n_grid - 1)
    def _():
        for cp in copies(n_waves - 1, 0):
            cp.wait()


def _prompt_attention_and_sample_gate(qt, krm, vt, kmean, page_table, kc, qcol, kcol,
                                      *, bsz, seq, past_len, pages_per_wave):
    d_attn = qt.shape[1]
    nb = seq // MOBA_BLOCK
    assert nb <= SUBLANES, "selection-bias channels hold one block per row of a vector register"
    n_pairs = d_attn // (2 * HEAD_DIM)
    n_heads = d_attn // HEAD_DIM
    pw = 2 * HEAD_DIM
    n_grid = bsz * n_pairs
    n_seq, pages_per_seq = page_table.shape
    pages_per_step = n_seq * pages_per_seq // n_grid
    assert pages_per_step * n_grid == n_seq * pages_per_seq and pages_per_seq % pages_per_step == 0
    waves_per_step = pages_per_step // pages_per_wave
    assert waves_per_step * pages_per_wave == pages_per_step and waves_per_step % 2 == 0
    assert (2 * nb) % waves_per_step == 0
    assert pages_per_wave % ((MOBA_BLOCK // LANES) * (2 * nb // waves_per_step)) == 0
    gate_rows = ((pages_per_seq * LANES // MOBA_BLOCK + 1 + SUBLANES - 1) // SUBLANES) * SUBLANES
    seq_of = lambda b, p: (b * n_pairs + p) * pages_per_step // pages_per_seq
    col_spec = pl.BlockSpec((1, d_attn, 1), lambda b, p, pt: (seq_of(b, p), 0, 0))
    kern = functools.partial(_attn_kernel, nb=nb, n_grid=n_grid, waves_per_step=waves_per_step,
                             pages_per_wave=pages_per_wave, pages_per_seq=pages_per_seq, n_heads=n_heads,
                             past_len=past_len)
    grid_spec = pltpu.PrefetchScalarGridSpec(
        num_scalar_prefetch=1,
        grid=(bsz, n_pairs),
        in_specs=[
            pl.BlockSpec((1, pw, seq), lambda b, p, pt: (b, p, 0)),
            pl.BlockSpec((seq, pw), lambda b, p, pt: (b, p)),
            pl.BlockSpec((seq, pw), lambda b, p, pt: (0, 0)),
            pl.BlockSpec((1, pw, seq), lambda b, p, pt: (b, p, 0)),
            pl.BlockSpec((1, nb, pw), lambda b, p, pt: (b, 0, p)),
            pl.BlockSpec(memory_space=pl.ANY),
            col_spec, col_spec,
        ],
        out_specs=[
            pl.BlockSpec((seq, pw), lambda b, p, pt: (b, p)),
            pl.BlockSpec((1, SUBLANES, LANES), lambda b, p, pt: (seq_of(b, p), 0, 0)),
        ],
        scratch_shapes=[
            pltpu.VMEM((seq, 2 * pw), BF16),
            pltpu.VMEM((pw, seq), BF16),
            pltpu.VMEM((2, pages_per_wave, d_attn, LANES), F32),
            pltpu.VMEM((n_heads, gate_rows, LANES), F32),
            pltpu.SemaphoreType.DMA((2,)),
        ],
    )
    return pl.pallas_call(
        kern,
        grid_spec=grid_spec,
        out_shape=[jax.ShapeDtypeStruct((bsz * seq, d_attn), BF16),
                   jax.ShapeDtypeStruct((n_seq, SUBLANES, LANES), jnp.int32)],
        compiler_params=pltpu.CompilerParams(
            dimension_semantics=("arbitrary", "arbitrary"), vmem_limit_bytes=48 * MIB),
        name="prompt_moba_attention",
    )(page_table, qt, krm, _key_bias_channels(seq), vt, kmean, kc, qcol, kcol)


def _ffn_kernel(x_ref, att_ref, cm_ref, wo_ref, g1_ref, b1_ref, wup_ref, wfc_ref, wdn_ref, g2_ref, b2_ref,
                y_ref, fst_ref,
                x1_ref, x1b_ref, acc_ref, ua_ref, ug_ref, carry_ref,
                *, tm, tiles_per_batch, d_attn, d_ff, cw, alpha):
    t = pl.program_id(0)
    mix = _dot(att_ref[...], wo_ref[0:d_attn, :]) + _dot(cm_ref[...], wo_ref[d_attn:, :])
    x1 = _layer_norm(alpha * x_ref[...] + mix, g1_ref[...], b1_ref[...])
    x1_ref[...] = x1
    x1b_ref[...] = x1.astype(BF16)

    @pl.when(t % tiles_per_batch == 0)
    def _():
        carry_ref[...] = jnp.zeros(carry_ref.shape, F32)

    def up_project(col0, ubuf):
        ubuf[0:SUBLANES, :] = carry_ref[:, col0:col0 + cw]
        ubuf[SUBLANES:SUBLANES + tm, :] = _dot(x1b_ref[...], wup_ref[:, col0:col0 + cw])

    def token_conv(col0, ubuf):
        w = wfc_ref[:, col0:col0 + cw]
        conv = (w[0:1, :] * ubuf[SUBLANES - 2:SUBLANES - 2 + tm, :]
                + w[1:2, :] * ubuf[SUBLANES - 1:SUBLANES - 1 + tm, :]
                + w[2:3, :] * ubuf[SUBLANES:SUBLANES + tm, :])
        carry_ref[:, col0:col0 + cw] = ubuf[tm:tm + SUBLANES, :]
        fst_ref[0, :, col0:col0 + cw] = ubuf[SUBLANES + tm - (CONV_W - 1):SUBLANES + tm, :]
        return conv

    n_chunks = d_ff // cw
    hcol = {}
    for c in range(-1, n_chunks + 2):
        if 0 <= c + 1 < n_chunks:
            up_project((c + 1) * cw, ua_ref.at[(c + 1) % 2])
            up_project(d_ff + (c + 1) * cw, ug_ref.at[(c + 1) % 2])
        if 0 <= c < n_chunks:
            a = token_conv(c * cw, ua_ref.at[c % 2])
            g = token_conv(d_ff + c * cw, ug_ref.at[c % 2])
            hcol[c] = (_silu(a) * g).astype(BF16)
        if c >= 2:
            part = _dot(hcol.pop(c - 2), wdn_ref[(c - 2) * cw:(c - 1) * cw, :])
            if c == 2:
                acc_ref[...] = part
            else:
                acc_ref[...] += part

    y_ref[...] = _layer_norm(alpha * x1_ref[...] + acc_ref[...], g2_ref[...], b2_ref[...])


def _prompt_ffn(x2d, att, cm, wo, g1, b1, wup, wfc, wdn, g2, b2, *, bsz, seq, tm, cw, alpha):
    n, d_model = x2d.shape
    d_attn = att.shape[1]
    d_ff = wdn.shape[0]
    tpb = seq // tm
    kern = functools.partial(_ffn_kernel, tm=tm, tiles_per_batch=tpb, d_attn=d_attn, d_ff=d_ff, cw=cw,
                             alpha=alpha)
    const = lambda t: (0, 0)
    resident = lambda a: pl.BlockSpec(a.shape, const, pipeline_mode=pl.Buffered(1))
    row = lambda w: pl.BlockSpec((tm, w), lambda t: (t, 0))
    return pl.pallas_call(
        kern,
        grid=(n // tm,),
        in_specs=[row(d_model), row(d_attn), row(cm.shape[1]),
                  resident(wo), resident(g1), resident(b1), resident(wup), resident(wfc), resident(wdn),
                  resident(g2), resident(b2)],
        out_specs=[row(d_model),
                   pl.BlockSpec((1, CONV_W - 1, 2 * d_ff), lambda t: (t // tpb, 0, 0))],
        out_shape=[jax.ShapeDtypeStruct((n, d_model), F32),
                   jax.ShapeDtypeStruct((bsz, CONV_W - 1, 2 * d_ff), F32)],
        scratch_shapes=[
            pltpu.VMEM((tm, d_model), F32),
            pltpu.VMEM((tm, d_model), BF16),
            pltpu.VMEM((tm, d_model), F32),
            pltpu.VMEM((2, tm + 2 * SUBLANES, cw), F32),
            pltpu.VMEM((2, tm + 2 * SUBLANES, cw), F32),
            pltpu.VMEM((SUBLANES, 2 * d_ff), F32),
        ],
        compiler_params=pltpu.CompilerParams(
            dimension_semantics=("arbitrary",), vmem_limit_bytes=56 * MIB),
        name="prompt_outproj_ffn",
    )(x2d, att, cm, wo, g1, b1, wup, wfc, wdn, g2, b2)


def _sample_inproj_kernel(x_ref, w_ref, wc_ref, st0_ref, st1_ref,
                          q_ref, k_ref, v_ref, cm_ref, u_ref, *, d_attn, d_conv):
    xb = x_ref[...].astype(BF16)
    z = _dot(xb, w_ref[...])
    q_ref[...] = z[:, 0:d_attn] * 0.125
    k_ref[...] = z[:, d_attn:2 * d_attn]
    v_ref[...] = z[:, 2 * d_attn:3 * d_attn]
    o = 3 * d_attn
    gb = z[:, o:o + d_conv]
    u = z[:, o + d_conv:o + 2 * d_conv] * z[:, o + 2 * d_conv:o + 3 * d_conv]
    wc = wc_ref[...]
    conv = wc[0:1, :] * st0_ref[...] + wc[1:2, :] * st1_ref[...] + wc[2:3, :] * u
    cm_ref[...] = gb * conv
    u_ref[...] = u


def _sample_inproj(xs, w_in_b, w_conv, st0, st1, *, d_attn, d_conv):
    b = xs.shape[0]
    kern = functools.partial(_sample_inproj_kernel, d_attn=d_attn, d_conv=d_conv)
    sd = lambda w: jax.ShapeDtypeStruct((b, w), F32)
    return pl.pallas_call(
        kern,
        out_shape=[sd(d_attn), sd(d_attn), sd(d_attn), sd(d_conv), sd(d_conv)],
        compiler_params=pltpu.CompilerParams(vmem_limit_bytes=32 * MIB),
        name="sample_inproj",
    )(xs, w_in_b, w_conv, st0, st1)


def _sel_copies(sel_ref, pt_ref, kc_ref, vc_ref, kbuf_ref, vbuf_ref, sem_ref, b, slot, *, n_heads):
    pages_per_block = MOBA_BLOCK // LANES
    copies = []
    for h in range(n_heads):
        for r in range(MOBA_TOPK):
            blk = sel_ref[(b * n_heads + h) * MOBA_TOPK + r]
            for pg in range(pages_per_block):
                phys = pt_ref[b, blk * pages_per_block + pg]
                rows = pl.ds(h * HEAD_DIM, HEAD_DIM)
                t = r * pages_per_block + pg
                copies.append(pltpu.make_async_copy(kc_ref.at[phys, rows], kbuf_ref.at[slot, h, t],
                                                    sem_ref.at[0, slot]))
                copies.append(pltpu.make_async_copy(vc_ref.at[phys, rows], vbuf_ref.at[slot, h, t],
                                                    sem_ref.at[1, slot]))
    return copies


def _sample_attn_kernel(sel_ref, pt_ref, kc_ref, vc_ref, qcol_ref, kcol_ref, vcol_ref, o_ref,
                        kbuf_ref, vbuf_ref, sem_ref, *, n_seq, n_heads, past_len):
    b = pl.program_id(0)
    slot = b % 2
    pages_per_block = MOBA_BLOCK // LANES
    n_tiles = MOBA_TOPK * pages_per_block
    copies = functools.partial(_sel_copies, sel_ref, pt_ref, kc_ref, vc_ref, kbuf_ref, vbuf_ref, sem_ref,
                               n_heads=n_heads)

    @pl.when(b == 0)
    def _():
        for cp in copies(b, slot):
            cp.start()

    @pl.when(b + 1 < n_seq)
    def _():
        for cp in copies(b + 1, 1 - slot):
            cp.start()

    for cp in copies(b, slot):
        cp.wait()

    qcol = qcol_ref[0]
    kcol = kcol_ref[0]
    vcol = vcol_ref[0]
    lane = lax.broadcasted_iota(jnp.int32, (1, LANES), 1)
    heads = range(n_heads)
    hs = [slice(h * HEAD_DIM, (h + 1) * HEAD_DIM) for h in heads]

    s_own, s_all = [], []
    for h in heads:
        qb = jnp.broadcast_to(qcol[hs[h], :], (HEAD_DIM, LANES))
        s_own.append(jnp.sum(qcol[hs[h], :] * kcol[hs[h], :], axis=0, keepdims=True))
        rows = []
        for t in range(n_tiles):
            blk = sel_ref[(b * n_heads + h) * MOBA_TOPK + t // pages_per_block]
            pos0 = blk * MOBA_BLOCK + (t % pages_per_block) * LANES
            dist = (past_len - pos0 - lane).astype(F32)
            rows.append(jnp.sum(qb * kbuf_ref[slot, h, t], axis=0, keepdims=True) - 2.0 ** -(h + 1) * dist)
        s_all.append(jnp.concatenate(rows, axis=0))
    m = [jnp.maximum(s_own[h], jnp.max(s_all[h], axis=(0, 1), keepdims=True)) for h in heads]
    pr = [jnp.exp(s_all[h] - m[h]) for h in heads]
    p_own = [jnp.exp(s_own[h] - m[h]) for h in heads]
    l = [p_own[h] + jnp.sum(pr[h], axis=(0, 1), keepdims=True) for h in heads]
    for h in heads:
        acc = p_own[h] * vcol[hs[h], :]
        wv = None
        for t in range(n_tiles):
            part = vbuf_ref[slot, h, t] * pr[h][t:t + 1, :]
            wv = part if wv is None else wv + part
        acc = acc + jnp.sum(wv, axis=1, keepdims=True)
        o_ref[0, hs[h], :] = acc * (1.0 / l[h])


def _sample_attention(sel_flat, page_table, kc, vc, qcol, kcol, vcol, *, n_heads, past_len):
    bsz = page_table.shape[0]
    d_attn = n_heads * HEAD_DIM
    n_tiles = MOBA_TOPK * (MOBA_BLOCK // LANES)
    kern = functools.partial(_sample_attn_kernel, n_seq=bsz, n_heads=n_heads, past_len=past_len)
    col_spec = pl.BlockSpec((1, d_attn, 1), lambda b, sel, pt: (b, 0, 0))
    any_spec = pl.BlockSpec(memory_space=pl.ANY)
    grid_spec = pltpu.PrefetchScalarGridSpec(
        num_scalar_prefetch=2,
        grid=(bsz,),
        in_specs=[any_spec, any_spec, col_spec, col_spec, col_spec],
        out_specs=col_spec,
        scratch_shapes=[
            pltpu.VMEM((2, n_heads, n_tiles, HEAD_DIM, LANES), F32),
            pltpu.VMEM((2, n_heads, n_tiles, HEAD_DIM, LANES), F32),
            pltpu.SemaphoreType.DMA((2, 2)),
        ],
    )
    return pl.pallas_call(
        kern,
        grid_spec=grid_spec,
        out_shape=jax.ShapeDtypeStruct((bsz, d_attn, 1), F32),
        compiler_params=pltpu.CompilerParams(
            dimension_semantics=("arbitrary",), vmem_limit_bytes=32 * MIB),
        name="sample_moba_attention",
    )(sel_flat, page_table, kc, vc, qcol, kcol, vcol)


def _sample_ffn_kernel(x_ref, att_ref, cm_ref, wo_ref, g1_ref, b1_ref, wup_ref, wfc_ref, sf0_ref, sf1_ref,
                       wdn_ref, g2_ref, b2_ref, y_ref, up_ref, *, d_attn, d_ff, alpha):
    mix = (_dot(att_ref[...].astype(BF16), wo_ref[0:d_attn, :])
           + _dot(cm_ref[...].astype(BF16), wo_ref[d_attn:, :]))
    x1 = _layer_norm(alpha * x_ref[...] + mix, g1_ref[...], b1_ref[...])
    up = _dot(x1.astype(BF16), wup_ref[...])
    up_ref[...] = up
    w = wfc_ref[...]
    conv = w[0:1, :] * sf0_ref[...] + w[1:2, :] * sf1_ref[...] + w[2:3, :] * up
    hcol = (_silu(conv[:, 0:d_ff]) * conv[:, d_ff:]).astype(BF16)
    f = _dot(hcol, wdn_ref[...])
    y_ref[...] = _layer_norm(alpha * x1 + f, g2_ref[...], b2_ref[...])


def _sample_ffn(xs, att, cm, wo, g1, b1, wup, wfc, sf0, sf1, wdn, g2, b2, *, alpha):
    b, d_model = xs.shape
    d_ff = wdn.shape[0]
    kern = functools.partial(_sample_ffn_kernel, d_attn=att.shape[1], d_ff=d_ff, alpha=alpha)
    return pl.pallas_call(
        kern,
        out_shape=[jax.ShapeDtypeStruct((b, d_model), F32), jax.ShapeDtypeStruct((b, 2 * d_ff), F32)],
        compiler_params=pltpu.CompilerParams(vmem_limit_bytes=48 * MIB),
        name="sample_outproj_ffn",
    )(xs, att, cm, wo, g1, b1, wup, wfc, sf0, sf1, wdn, g2, b2)


def kernel(x_prompt, x_sample, cache_k, cache_v, state_conv, state_ffn_conv, page_table, w_in, w_conv, w_out,
           ln1_g, ln1_b, w_ffn_up, w_ffn_conv, w_ffn_down, ln2_g, ln2_b):
    depth = w_in.shape[0]
    assert depth == 1, "single-layer step"
    bsz, seq, d_model = x_prompt.shape
    dec_b, dec_t, _ = x_sample.shape
    assert dec_t == 1
    _, n_phys, page_size, n_heads, head_dim = cache_k.shape
    assert head_dim == HEAD_DIM and page_size == LANES
    d_attn = n_heads * head_dim
    d_conv = (w_in.shape[2] - 3 * d_attn) // 3
    d_ff = w_ffn_down.shape[1]
    past_len = page_table.shape[1] * page_size
    alpha = (2.0 * depth) ** 0.25
    row = lambda a: a.reshape(1, -1)

    w_in_b = w_in[0].astype(BF16)
    wt = w_in_b[:, :3 * d_attn].T
    wr = w_in_b[:, d_attn:2 * d_attn]
    wr = jnp.concatenate([wr, w_in_b[:, 3 * d_attn:]], axis=1)
    wo = w_out[0].astype(BF16)
    wup = w_ffn_up[0].astype(BF16)
    wdn = w_ffn_down[0].astype(BF16)
    wc, wfc = w_conv[0], w_ffn_conv[0]
    g1, b1, g2, b2 = row(ln1_g[0]), row(ln1_b[0]), row(ln2_g[0]), row(ln2_b[0])

    xs = x_sample.reshape(dec_b, d_model)
    st0, st1 = state_conv[0, :, 0, :], state_conv[0, :, 1, :]
    q_s, k_s, v_s, cm_s, u_s = _sample_inproj(xs, w_in_b, wc, st0, st1, d_attn=d_attn, d_conv=d_conv)
    kc = jnp.transpose(cache_k[0], (0, 2, 3, 1)).reshape(n_phys, d_attn, page_size)
    vc = jnp.transpose(cache_v[0], (0, 2, 3, 1)).reshape(n_phys, d_attn, page_size)
    qcol, kcol, vcol = q_s[:, :, None], k_s[:, :, None], v_s[:, :, None]

    x2d = x_prompt.reshape(bsz * seq, d_model)
    qt, kt, vt, krm, kmean, cm, conv_p = _prompt_inproj(x2d, wt, wr, wc, bsz=bsz, seq=seq, tm=512)
    att, sel = _prompt_attention_and_sample_gate(qt, krm, vt, kmean, page_table, kc, qcol, kcol,
                                                 bsz=bsz, seq=seq, past_len=past_len, pages_per_wave=16)
    y_p, ffn_p = _prompt_ffn(x2d, att, cm, wo, g1, b1, wup, wfc, wdn, g2, b2,
                             bsz=bsz, seq=seq, tm=512, cw=256, alpha=alpha)
    to_cache = lambda a: jnp.transpose(a.reshape(1, bsz, n_heads, head_dim, seq), (0, 1, 4, 2, 3))
    k_prompt, v_prompt = to_cache(kt), to_cache(vt)

    sel_flat = jnp.transpose(sel[:, :MOBA_TOPK, :n_heads], (0, 2, 1)).reshape(-1)
    att_s = _sample_attention(sel_flat, page_table, kc, vc, qcol, kcol, vcol,
                              n_heads=n_heads, past_len=past_len)
    sf0, sf1 = state_ffn_conv[0, :, 0, :], state_ffn_conv[0, :, 1, :]
    y_s, up_s = _sample_ffn(xs, att_s.reshape(dec_b, d_attn), cm_s, wo, g1, b1, wup, wfc, sf0, sf1,
                            wdn, g2, b2, alpha=alpha)

    return (y_p.reshape(bsz, seq, d_model),
            y_s.reshape(dec_b, 1, d_model),
            k_prompt, v_prompt,
            conv_p[None], ffn_p[None],
            k_s.reshape(1, dec_b, 1, n_heads, head_dim), v_s.reshape(1, dec_b, 1, n_heads, head_dim),
            jnp.stack([st1, u_s], axis=1)[None],
            jnp.stack([sf1, up_s], axis=1)[None])
```

```python
import functools

import jax
import jax.numpy as jnp
from jax import lax
from jax.experimental import pallas as pl
from jax.experimental.pallas import tpu as pltpu

F32 = jnp.float32
BF16 = jnp.bfloat16

HEAD_DIM = 64
MOBA_BLOCK = 256
MOBA_TOPK = 3
CONV_W = 3
LN_EPS = 1e-5
LOG2E = 1.4426950408889634
NEG = -1e30
SUBLANES = 8
LANES = 128
MIB = 1024 * 1024

_NT = (((1,), (1,)), ((), ()))


def _dot(a, b):
    return jnp.dot(a, b, preferred_element_type=F32)


def _dot_nt(a, b):
    return lax.dot_general(a, b, _NT, preferred_element_type=F32)


def _layer_norm(x, g, b):
    mu = jnp.mean(x, axis=-1, keepdims=True)
    xc = x - mu
    var = jnp.mean(xc * xc, axis=-1, keepdims=True)
    return xc * lax.rsqrt(var + LN_EPS) * g + b


def _silu(a):
    return a * jax.nn.sigmoid(a)


def _inproj_kernel(x_ref, wt_ref, wr_ref, wc_ref,
                   qt_ref, kt_ref, vt_ref, krm_ref, kmean_ref, cm_ref, cst_ref,
                   ubuf_ref, *, tm, tiles_per_batch, d_attn, d_conv):
    t = pl.program_id(0)
    tb = t % tiles_per_batch
    @pl.when(tb == 0)
    def _():
        ubuf_ref[0:SUBLANES, :] = jnp.zeros((SUBLANES, d_conv), F32)

    xb = x_ref[...].astype(BF16)

    gc = _dot(xb, wr_ref[:, d_attn + d_conv:d_attn + 2 * d_conv])
    hh = _dot(xb, wr_ref[:, d_attn + 2 * d_conv:d_attn + 3 * d_conv])
    u = gc * hh
    gb = _dot(xb, wr_ref[:, d_attn:d_attn + d_conv])
    ubuf_ref[SUBLANES:SUBLANES + tm, :] = u
    wc = wc_ref[...]
    conv = (wc[0:1, :] * ubuf_ref[SUBLANES - 2:SUBLANES - 2 + tm, :]
            + wc[1:2, :] * ubuf_ref[SUBLANES - 1:SUBLANES - 1 + tm, :]
            + wc[2:3, :] * u)
    cm_ref[...] = (gb * conv).astype(BF16)
    cst_ref[0] = ubuf_ref[SUBLANES + tm - (CONV_W - 1):SUBLANES + tm, :]
    ubuf_ref[0:SUBLANES, :] = ubuf_ref[tm:tm + SUBLANES, :]

    k = _dot(xb, wr_ref[:, 0:d_attn])
    kt_ref[0] = k.T
    krm_ref[...] = k.astype(BF16)
    for blk in range(tm // MOBA_BLOCK):
        ksum = jnp.sum(k[blk * MOBA_BLOCK:(blk + 1) * MOBA_BLOCK, :], axis=0, keepdims=True)
        kmean_ref[0, pl.ds(tb * (tm // MOBA_BLOCK) + blk, 1), :] = ksum * (1.0 / MOBA_BLOCK)

    qt_ref[0] = (_dot_nt(wt_ref[0:d_attn, :], xb) * (LOG2E / HEAD_DIM ** 0.5)).astype(BF16)
    vt_ref[0] = _dot_nt(wt_ref[2 * d_attn:3 * d_attn, :], xb)


def _prompt_inproj(x2d, wt, wr, w_conv, *, bsz, seq, tm):
    n, d_model = x2d.shape
    d_attn = wt.shape[0] // 3
    d_conv = (wr.shape[1] - d_attn) // 3
    tpb = seq // tm
    nb = seq // MOBA_BLOCK
    kern = functools.partial(_inproj_kernel, tm=tm, tiles_per_batch=tpb, d_attn=d_attn, d_conv=d_conv)
    const = lambda t: (0, 0)
    ct_spec = lambda: pl.BlockSpec((1, d_attn, tm), lambda t: (t // tpb, 0, t % tpb))
    return pl.pallas_call(
        kern,
        grid=(n // tm,),
        in_specs=[
            pl.BlockSpec((tm, d_model), lambda t: (t, 0)),
            pl.BlockSpec(wt.shape, const),
            pl.BlockSpec(wr.shape, const),
            pl.BlockSpec(w_conv.shape, const),
        ],
        out_specs=[
            ct_spec(), ct_spec(), ct_spec(),
            pl.BlockSpec((tm, d_attn), lambda t: (t, 0)),
            pl.BlockSpec((1, nb, d_attn), lambda t: (t // tpb, 0, 0)),
            pl.BlockSpec((tm, d_conv), lambda t: (t, 0)),
            pl.BlockSpec((1, CONV_W - 1, d_conv), lambda t: (t // tpb, 0, 0)),
        ],
        out_shape=[
            jax.ShapeDtypeStruct((bsz, d_attn, seq), BF16),
            jax.ShapeDtypeStruct((bsz, d_attn, seq), F32),
            jax.ShapeDtypeStruct((bsz, d_attn, seq), F32),
            jax.ShapeDtypeStruct((n, d_attn), BF16),
            jax.ShapeDtypeStruct((bsz, nb, d_attn), F32),
            jax.ShapeDtypeStruct((n, d_conv), BF16),
            jax.ShapeDtypeStruct((bsz, CONV_W - 1, d_conv), F32),
        ],
        scratch_shapes=[pltpu.VMEM((tm + 2 * SUBLANES, d_conv), F32)],
        compiler_params=pltpu.CompilerParams(
            dimension_semantics=("arbitrary",), vmem_limit_bytes=48 * MIB),
        name="prompt_inproj",
    )(x2d, wt, wr, w_conv)


BIAS_CH0 = 16
GATE_PREFETCH = 2
DENOM_ROWS = 16


def _key_bias_channels(seq):
    pos = jnp.arange(seq, dtype=jnp.int32)[:, None]
    c = jnp.arange(2 * HEAD_DIM, dtype=jnp.int32)[None, :] - BIAS_CH0
    kblk, koff = pos // MOBA_BLOCK, pos % MOBA_BLOCK
    ext = jnp.where(c + BIAS_CH0 < SUBLANES, (kblk == c + BIAS_CH0).astype(F32), 0.0)
    ext = jnp.where((c >= 0) & (c < 3), koff.astype(F32), ext)
    ext = jnp.where((c >= 3) & (c < 6), (kblk * MOBA_BLOCK).astype(F32), ext)
    ext = jnp.where((c >= 6) & (c < 12), 1.0, ext)
    return ext.astype(BF16)


def _query_alibi_channels(coef, block, nq):
    row = lax.broadcasted_iota(jnp.int32, (2 * SUBLANES, nq), 0)
    qoff = lax.broadcasted_iota(jnp.int32, (2 * SUBLANES, nq), 1).astype(F32)
    cb = jnp.full((2 * SUBLANES, nq), 1.0, F32) * coef
    val = jnp.where(row < 6, cb,
                    jnp.where(row < 9, -cb * qoff,
                              jnp.where(row < 12, -cb * float(block * MOBA_BLOCK), 0.0)))
    hi = val.astype(BF16).astype(F32)
    mid = (val - hi).astype(BF16).astype(F32)
    lo = ((val - hi) - mid).astype(BF16).astype(F32)
    part = row % 3
    return jnp.where(part == 0, hi, jnp.where(part == 1, mid, lo))


def _selection_bias(g, n_past):
    sub = lax.broadcasted_iota(jnp.int32, g.shape, 0)
    rank = jnp.zeros(g.shape, jnp.int32)
    for jp in range(n_past):
        row = g[jp:jp + 1, :]
        beats = (row > g) | ((row == g) & (jp < sub))
        rank = rank + jnp.where(beats, 1, 0)
    sel = ((sub < n_past) & (rank < MOBA_TOPK)) | (sub == n_past)
    return jnp.where(sel, 0.0, NEG).astype(F32)


def _gate_wave_copies(pt_ref, kc_ref, buf_ref, sem_ref, wave, slot, *, pages_per_wave, pages_per_seq):
    first = wave * pages_per_wave
    seq_idx = first // pages_per_seq
    pg0 = first % pages_per_seq
    return [pltpu.make_async_copy(kc_ref.at[pt_ref[seq_idx, pg0 + pg]], buf_ref.at[slot, pg], sem_ref.at[slot])
            for pg in range(pages_per_wave)]


def _sample_gate_wave(copies, wave, w, part, n_parts, n_waves, qcol_ref, buf_ref, g_ref,
                      *, pages_per_wave, pages_per_seq, n_heads):
    slot = w
    n_slots = buf_ref.shape[0]
    pages_per_block = MOBA_BLOCK // LANES
    blocks_per_part = pages_per_wave // pages_per_block // n_parts
    n_blocks = pages_per_seq // pages_per_block
    row0 = (wave * pages_per_wave % pages_per_seq) // pages_per_block

    if part == 0:
        for cp in copies(jnp.minimum(wave + GATE_PREFETCH, n_waves - 1), (w + GATE_PREFETCH) % n_slots):
            cp.start()
        for cp in copies(wave, slot):
            cp.wait()
        g_ref[:, n_blocks:, :] = jnp.zeros((n_heads, g_ref.shape[1] - n_blocks, LANES), F32)

    qcol = qcol_ref[0]
    for h in range(n_heads):
        qb = jnp.broadcast_to(qcol[h * HEAD_DIM:(h + 1) * HEAD_DIM, :], (HEAD_DIM, LANES))
        for bl in range(part * blocks_per_part, (part + 1) * blocks_per_part):
            ksum = None
            for pg in range(pages_per_block):
                kt = buf_ref[slot, bl * pages_per_block + pg, h * HEAD_DIM:(h + 1) * HEAD_DIM, :]
                ksum = kt if ksum is None else ksum + kt
            g_ref[h, pl.ds(row0 + bl, 1), :] = jnp.sum(qb * ksum, axis=0, keepdims=True)


def _sample_gate_topk(qcol_ref, kcol_ref, sel_ref, g_ref, *, n_heads, past_len):
    qcol = qcol_ref[0]
    kcol = kcol_ref[0]
    own = past_len // MOBA_BLOCK
    nrow = g_ref.shape[1]
    lane = lax.broadcasted_iota(jnp.int32, (nrow, LANES), 1)
    sub = lax.broadcasted_iota(jnp.int32, (nrow, LANES), 0)
    gate = jnp.full((nrow, LANES), -jnp.inf, F32)
    for h in range(n_heads):
        tot = jnp.sum(g_ref[h], axis=1, keepdims=True)
        own_dot = jnp.sum(qcol[h * HEAD_DIM:(h + 1) * HEAD_DIM, :] * kcol[h * HEAD_DIM:(h + 1) * HEAD_DIM, :],
                          axis=0, keepdims=True)
        tot = jnp.where(sub[:, 0:1] == own, own_dot, tot) * (1.0 / MOBA_BLOCK)
        gate = jnp.where(lane == h, tot, gate)
    gate = jnp.where(sub < own, gate, -jnp.inf)
    rows = []
    for _ in range(MOBA_TOPK):
        m = jnp.max(gate, axis=0, keepdims=True)
        idx = jnp.min(jnp.where(gate == m, sub, nrow), axis=0, keepdims=True)
        rows.append(idx)
        gate = jnp.where(sub == idx, -jnp.inf, gate)
    rows.append(jnp.zeros((SUBLANES - MOBA_TOPK, LANES), jnp.int32))
    sel_ref[0] = jnp.concatenate(rows, axis=0)


def _attn_kernel(pt_ref, qt_ref, k_ref, kx_ref, vt_ref, kmean_ref, kc_ref, qcol_ref, kcol_ref,
                 o_ref, sel_ref,
                 kk_ref, vtb_ref, pbuf_ref, g_ref, sem_ref,
                 *, nb, n_grid, waves_per_step, pages_per_wave, pages_per_seq, n_heads, past_len):
    pair = pl.program_id(1)
    step = pl.program_id(0) * pl.num_programs(1) + pair
    blk = MOBA_BLOCK
    pw = 2 * HEAD_DIM

    n_waves = n_grid * waves_per_step
    copies = functools.partial(_gate_wave_copies, pt_ref, kc_ref, pbuf_ref, sem_ref,
                               pages_per_wave=pages_per_wave, pages_per_seq=pages_per_seq)
    gate_wave = functools.partial(_sample_gate_wave, copies, qcol_ref=qcol_ref, buf_ref=pbuf_ref, g_ref=g_ref,
                                  n_waves=n_waves, pages_per_wave=pages_per_wave, pages_per_seq=pages_per_seq,
                                  n_heads=n_heads)

    @pl.when(step == 0)
    def _():
        for ahead in range(GATE_PREFETCH):
            for cp in copies(ahead, ahead):
                cp.start()

    kk_ref[:, 0:pw] = k_ref[...]
    kk_ref[:, pw:2 * pw] = kx_ref[...]
    vrows = HEAD_DIM + DENOM_ROWS
    ones_row = jnp.where(lax.broadcasted_iota(jnp.int32, (DENOM_ROWS, vt_ref.shape[2]), 0) == 0,
                         1.0, 0.0).astype(BF16)
    for e in range(2):
        vtb_ref[e * vrows:e * vrows + HEAD_DIM, :] = vt_ref[0, e * HEAD_DIM:(e + 1) * HEAD_DIM, :].astype(BF16)
        vtb_ref[e * vrows + HEAD_DIM:(e + 1) * vrows, :] = ones_row

    slope_even = jnp.where(pair == 0, 2.0 ** -1,
                           jnp.where(pair == 1, 2.0 ** -3,
                                     jnp.where(pair == 2, 2.0 ** -5, 2.0 ** -7))).astype(F32)
    causal = (lax.broadcasted_iota(jnp.int32, (blk, blk), 1)
              >= lax.broadcasted_iota(jnp.int32, (blk, blk), 0))
    zeros8 = jnp.zeros((SUBLANES, blk), F32)
    zeros_h = jnp.zeros((HEAD_DIM, blk), BF16)
    zeros_tail = jnp.zeros((pw - 2 * BIAS_CH0, blk), BF16)
    kmean = kmean_ref[0]

    def scores(i, e):
        sl = slope_even * (0.5 ** e)
        qh = qt_ref[0, e * HEAD_DIM:(e + 1) * HEAD_DIM, i * blk:(i + 1) * blk]
        if i > 0:
            gate = jnp.dot(kmean[:, e * HEAD_DIM:(e + 1) * HEAD_DIM], qh.astype(F32),
                           preferred_element_type=F32, precision=lax.Precision.HIGHEST)
        else:
            gate = zeros8
        selb = _selection_bias(gate, i)
        q_ext = jnp.concatenate([qh, zeros_h] if e == 0 else [zeros_h, qh], axis=0)
        q_ext = jnp.concatenate(
            [q_ext,
             jnp.concatenate([selb, zeros8], axis=0).astype(BF16),
             _query_alibi_channels(sl * LOG2E, i, blk).astype(BF16),
             zeros_tail], axis=0)
        s = _dot(kk_ref[0:(i + 1) * blk, :], q_ext)
        s_own = jnp.where(causal, s[i * blk:, :], NEG)
        return s_own if i == 0 else jnp.concatenate([s[0:i * blk, :], s_own], axis=0)

    def softmax(s):
        m = jnp.max(s, axis=0, keepdims=True)
        return (jnp.exp2(s - m).astype(BF16),)

    def weighted_values(i, e, pr):
        acc = _dot(vtb_ref[e * vrows:(e + 1) * vrows, 0:(i + 1) * blk], pr)
        return acc[0:HEAD_DIM, :] * (1.0 / acc[HEAD_DIM:HEAD_DIM + 1, :])

    order = [(i, e) for i in range(nb) for e in range(2)]
    n_steps = len(order)
    s_buf, p_buf, outs = {}, {}, []
    parts_per_wave = n_steps // waves_per_step
    work = [i + 1 for i, _ in order]
    gate_parts_at = {n: [] for n in range(n_steps)}
    for k in range(n_steps):
        target = (k + 0.5) * sum(work) / n_steps
        gate_parts_at[next(n for n in range(n_steps) if sum(work[:n + 1]) >= target)].append(k)
    for n in range(-1, n_steps + 1):
        for k in gate_parts_at.get(n + 1, ()):
            w, part = divmod(k, parts_per_wave)
            gate_wave(step * waves_per_step + w, w, part, parts_per_wave)
        if 0 <= n + 1 < n_steps:
            s_buf[n + 1] = scores(*order[n + 1])
        if 0 <= n < n_steps:
            p_buf[n] = softmax(s_buf.pop(n))
        if n >= 1:
            i, e = order[n - 1]
            outs.append(weighted_values(i, e, *p_buf.pop(n - 1)))
            if e == 1:
                o_ref[i * blk:(i + 1) * blk, :] = jnp.concatenate(outs, axis=0).T.astype(BF16)
                outs = []

    @pl.when((step + 1) * waves_per_step * pages_per_wave % pages_per_seq == 0)
    def _():
        _sample_gate_topk(qcol_ref, kcol_ref, sel_ref, g_ref, n_heads=n_heads, past_len=past_len)

    @pl.when(step == n_grid - 1)
    def _():
        for ahead in range(GATE_PREFETCH):
            for cp in copies(n_waves - 1, ahead):
                cp.wait()


def _prompt_attention_and_sample_gate(qt, krm, vt, kmean, page_table, kc, qcol, kcol,
                                      *, bsz, seq, past_len, pages_per_wave):
    d_attn = qt.shape[1]
    nb = seq // MOBA_BLOCK
    assert nb <= SUBLANES, "selection-bias channels hold one block per row of a vector register"
    n_pairs = d_attn // (2 * HEAD_DIM)
    n_heads = d_attn // HEAD_DIM
    pw = 2 * HEAD_DIM
    n_grid = bsz * n_pairs
    n_seq, pages_per_seq = page_table.shape
    pages_per_step = n_seq * pages_per_seq // n_grid
    assert pages_per_step * n_grid == n_seq * pages_per_seq and pages_per_seq % pages_per_step == 0
    waves_per_step = pages_per_step // pages_per_wave
    assert waves_per_step * pages_per_wave == pages_per_step and waves_per_step > GATE_PREFETCH
    assert (2 * nb) % waves_per_step == 0
    assert pages_per_wave % ((MOBA_BLOCK // LANES) * (2 * nb // waves_per_step)) == 0
    gate_rows = ((pages_per_seq * LANES // MOBA_BLOCK + 1 + SUBLANES - 1) // SUBLANES) * SUBLANES
    seq_of = lambda b, p: (b * n_pairs + p) * pages_per_step // pages_per_seq
    col_spec = pl.BlockSpec((1, d_attn, 1), lambda b, p, pt: (seq_of(b, p), 0, 0))
    kern = functools.partial(_attn_kernel, nb=nb, n_grid=n_grid, waves_per_step=waves_per_step,
                             pages_per_wave=pages_per_wave, pages_per_seq=pages_per_seq, n_heads=n_heads,
                             past_len=past_len)
    grid_spec = pltpu.PrefetchScalarGridSpec(
        num_scalar_prefetch=1,
        grid=(bsz, n_pairs),
        in_specs=[
            pl.BlockSpec((1, pw, seq), lambda b, p, pt: (b, p, 0)),
            pl.BlockSpec((seq, pw), lambda b, p, pt: (b, p)),
            pl.BlockSpec((seq, pw), lambda b, p, pt: (0, 0)),
            pl.BlockSpec((1, pw, seq), lambda b, p, pt: (b, p, 0)),
            pl.BlockSpec((1, nb, pw), lambda b, p, pt: (b, 0, p)),
            pl.BlockSpec(memory_space=pl.ANY),
            col_spec, col_spec,
        ],
        out_specs=[
            pl.BlockSpec((seq, pw), lambda b, p, pt: (b, p)),
            pl.BlockSpec((1, SUBLANES, LANES), lambda b, p, pt: (seq_of(b, p), 0, 0)),
        ],
        scratch_shapes=[
            pltpu.VMEM((seq, 2 * pw), BF16),
            pltpu.VMEM((pw + 2 * DENOM_ROWS, seq), BF16),
            pltpu.VMEM((waves_per_step, pages_per_wave, d_attn, LANES), F32),
            pltpu.VMEM((n_heads, gate_rows, LANES), F32),
            pltpu.SemaphoreType.DMA((waves_per_step,)),
        ],
    )
    return pl.pallas_call(
        kern,
        grid_spec=grid_spec,
        out_shape=[jax.ShapeDtypeStruct((bsz * seq, d_attn), BF16),
                   jax.ShapeDtypeStruct((n_seq, SUBLANES, LANES), jnp.int32)],
        compiler_params=pltpu.CompilerParams(
            dimension_semantics=("arbitrary", "arbitrary"), vmem_limit_bytes=48 * MIB),
        name="prompt_moba_attention",
    )(page_table, qt, krm, _key_bias_channels(seq), vt, kmean, kc, qcol, kcol)


def _ffn_kernel(x_ref, att_ref, cm_ref, wo_ref, g1_ref, b1_ref, wup_ref, wfc_ref, wdn_ref, g2_ref, b2_ref,
                y_ref, fst_ref,
                x1_ref, x1b_ref, acc_ref, ua_ref, ug_ref, carry_ref,
                *, tm, tiles_per_batch, d_attn, d_ff, cw, alpha):
    t = pl.program_id(0)
    mix = _dot(att_ref[...], wo_ref[0:d_attn, :]) + _dot(cm_ref[...], wo_ref[d_attn:, :])
    x1 = _layer_norm(alpha * x_ref[...] + mix, g1_ref[...], b1_ref[...])
    x1_ref[...] = x1
    x1b_ref[...] = x1.astype(BF16)

    @pl.when(t % tiles_per_batch == 0)
    def _():
        carry_ref[...] = jnp.zeros(carry_ref.shape, F32)

    def up_project(col0, ubuf):
        ubuf[0:SUBLANES, :] = carry_ref[:, col0:col0 + cw]
        ubuf[SUBLANES:SUBLANES + tm, :] = _dot(x1b_ref[...], wup_ref[:, col0:col0 + cw])

    def token_conv(col0, ubuf):
        w = wfc_ref[:, col0:col0 + cw]
        conv = (w[0:1, :] * ubuf[SUBLANES - 2:SUBLANES - 2 + tm, :]
                + w[1:2, :] * ubuf[SUBLANES - 1:SUBLANES - 1 + tm, :]
                + w[2:3, :] * ubuf[SUBLANES:SUBLANES + tm, :])
        carry_ref[:, col0:col0 + cw] = ubuf[tm:tm + SUBLANES, :]
        fst_ref[0, :, col0:col0 + cw] = ubuf[SUBLANES + tm - (CONV_W - 1):SUBLANES + tm, :]
        return conv

    n_chunks = d_ff // cw
    hcol = {}
    for c in range(-1, n_chunks + 2):
        if 0 <= c + 1 < n_chunks:
            up_project((c + 1) * cw, ua_ref.at[(c + 1) % 2])
        if 0 <= c < n_chunks:
            a = _silu(token_conv(c * cw, ua_ref.at[c % 2]))
        if 0 <= c + 1 < n_chunks:
            up_project(d_ff + (c + 1) * cw, ug_ref.at[(c + 1) % 2])
        if 0 <= c < n_chunks:
            hcol[c] = (a * token_conv(d_ff + c * cw, ug_ref.at[c % 2])).astype(BF16)
        if c >= 2:
            part = _dot(hcol.pop(c - 2), wdn_ref[(c - 2) * cw:(c - 1) * cw, :])
            if c == 2:
                acc_ref[...] = part
            else:
                acc_ref[...] += part

    y_ref[...] = _layer_norm(alpha * x1_ref[...] + acc_ref[...], g2_ref[...], b2_ref[...])


def _prompt_ffn(x2d, att, cm, wo, g1, b1, wup, wfc, wdn, g2, b2, *, bsz, seq, tm, cw, alpha):
    n, d_model = x2d.shape
    d_attn = att.shape[1]
    d_ff = wdn.shape[0]
    tpb = seq // tm
    kern = functools.partial(_ffn_kernel, tm=tm, tiles_per_batch=tpb, d_attn=d_attn, d_ff=d_ff, cw=cw,
                             alpha=alpha)
    const = lambda t: (0, 0)
    resident = lambda a: pl.BlockSpec(a.shape, const, pipeline_mode=pl.Buffered(1))
    row = lambda w: pl.BlockSpec((tm, w), lambda t: (t, 0))
    return pl.pallas_call(
        kern,
        grid=(n // tm,),
        in_specs=[row(d_model), row(d_attn), row(cm.shape[1]),
                  resident(wo), resident(g1), resident(b1), resident(wup), resident(wfc), resident(wdn),
                  resident(g2), resident(b2)],
        out_specs=[row(d_model),
                   pl.BlockSpec((1, CONV_W - 1, 2 * d_ff), lambda t: (t // tpb, 0, 0))],
        out_shape=[jax.ShapeDtypeStruct((n, d_model), F32),
                   jax.ShapeDtypeStruct((bsz, CONV_W - 1, 2 * d_ff), F32)],
        scratch_shapes=[
            pltpu.VMEM((tm, d_model), F32),
            pltpu.VMEM((tm, d_model), BF16),
            pltpu.VMEM((tm, d_model), F32),
            pltpu.VMEM((2, tm + 2 * SUBLANES, cw), F32),
            pltpu.VMEM((2, tm + 2 * SUBLANES, cw), F32),
            pltpu.VMEM((SUBLANES, 2 * d_ff), F32),
        ],
        compiler_params=pltpu.CompilerParams(
            dimension_semantics=("arbitrary",), vmem_limit_bytes=56 * MIB),
        name="prompt_outproj_ffn",
    )(x2d, att, cm, wo, g1, b1, wup, wfc, wdn, g2, b2)


def _sample_inproj_kernel(x_ref, w_ref, wc_ref, st0_ref, st1_ref,
                          q_ref, k_ref, v_ref, cm_ref, u_ref, *, d_attn, d_conv):
    z = jnp.dot(x_ref[...], w_ref[...], preferred_element_type=F32,
                precision=lax.Precision.HIGHEST)
    q_ref[...] = z[:, 0:d_attn] * 0.125
    k_ref[...] = z[:, d_attn:2 * d_attn]
    v_ref[...] = z[:, 2 * d_attn:3 * d_attn]
    o = 3 * d_attn
    gb = z[:, o:o + d_conv]
    u = z[:, o + d_conv:o + 2 * d_conv] * z[:, o + 2 * d_conv:o + 3 * d_conv]
    wc = wc_ref[...]
    conv = wc[0:1, :] * st0_ref[...] + wc[1:2, :] * st1_ref[...] + wc[2:3, :] * u
    cm_ref[...] = gb * conv
    u_ref[...] = u


def _sample_inproj(xs, w_in_f32, w_conv, st0, st1, *, d_attn, d_conv):
    b = xs.shape[0]
    kern = functools.partial(_sample_inproj_kernel, d_attn=d_attn, d_conv=d_conv)
    sd = lambda w: jax.ShapeDtypeStruct((b, w), F32)
    return pl.pallas_call(
        kern,
        out_shape=[sd(d_attn), sd(d_attn), sd(d_attn), sd(d_conv), sd(d_conv)],
        compiler_params=pltpu.CompilerParams(vmem_limit_bytes=48 * MIB),
        name="sample_inproj",
    )(xs, w_in_f32, w_conv, st0, st1)


def _sel_copies(sel_ref, pt_ref, kc_ref, vc_ref, kbuf_ref, vbuf_ref, sem_ref, b, slot, *, n_heads):
    pages_per_block = MOBA_BLOCK // LANES
    copies = []
    for h in range(n_heads):
        for r in range(MOBA_TOPK):
            blk = sel_ref[(b * n_heads + h) * MOBA_TOPK + r]
            for pg in range(pages_per_block):
                phys = pt_ref[b, blk * pages_per_block + pg]
                rows = pl.ds(h * HEAD_DIM, HEAD_DIM)
                t = r * pages_per_block + pg
                copies.append(pltpu.make_async_copy(kc_ref.at[phys, rows], kbuf_ref.at[slot, h, t],
                                                    sem_ref.at[0, slot]))
                copies.append(pltpu.make_async_copy(vc_ref.at[phys, rows], vbuf_ref.at[slot, h, t],
                                                    sem_ref.at[1, slot]))
    return copies


def _sample_attn_kernel(sel_ref, pt_ref, kc_ref, vc_ref, qcol_ref, kcol_ref, vcol_ref, o_ref,
                        kbuf_ref, vbuf_ref, sem_ref, *, n_seq, n_heads, past_len):
    b = pl.program_id(0)
    slot = b % 2
    pages_per_block = MOBA_BLOCK // LANES
    n_tiles = MOBA_TOPK * pages_per_block
    copies = functools.partial(_sel_copies, sel_ref, pt_ref, kc_ref, vc_ref, kbuf_ref, vbuf_ref, sem_ref,
                               n_heads=n_heads)

    @pl.when(b == 0)
    def _():
        for cp in copies(b, slot):
            cp.start()

    @pl.when(b + 1 < n_seq)
    def _():
        for cp in copies(b + 1, 1 - slot):
            cp.start()

    for cp in copies(b, slot):
        cp.wait()

    qcol = qcol_ref[0]
    kcol = kcol_ref[0]
    vcol = vcol_ref[0]
    lane = lax.broadcasted_iota(jnp.int32, (1, LANES), 1)
    heads = range(n_heads)
    hs = [slice(h * HEAD_DIM, (h + 1) * HEAD_DIM) for h in heads]

    s_own, s_all = [], []
    for h in heads:
        qb = jnp.broadcast_to(qcol[hs[h], :], (HEAD_DIM, LANES))
        s_own.append(jnp.sum(qcol[hs[h], :] * kcol[hs[h], :], axis=0, keepdims=True))
        rows = []
        for t in range(n_tiles):
            blk = sel_ref[(b * n_heads + h) * MOBA_TOPK + t // pages_per_block]
            pos0 = blk * MOBA_BLOCK + (t % pages_per_block) * LANES
            dist = (past_len - pos0 - lane).astype(F32)
            rows.append(jnp.sum(qb * kbuf_ref[slot, h, t], axis=0, keepdims=True) - 2.0 ** -(h + 1) * dist)
        s_all.append(jnp.concatenate(rows, axis=0))
    m = [jnp.maximum(s_own[h], jnp.max(s_all[h], axis=(0, 1), keepdims=True)) for h in heads]
    pr = [jnp.exp(s_all[h] - m[h]) for h in heads]
    p_own = [jnp.exp(s_own[h] - m[h]) for h in heads]
    l = [p_own[h] + jnp.sum(pr[h], axis=(0, 1), keepdims=True) for h in heads]
    for h in heads:
        acc = p_own[h] * vcol[hs[h], :]
        wv = None
        for t in range(n_tiles):
            part = vbuf_ref[slot, h, t] * pr[h][t:t + 1, :]
            wv = part if wv is None else wv + part
        acc = acc + jnp.sum(wv, axis=1, keepdims=True)
        o_ref[0, hs[h], :] = acc * (1.0 / l[h])


def _sample_attention(sel_flat, page_table, kc, vc, qcol, kcol, vcol, *, n_heads, past_len):
    bsz = page_table.shape[0]
    d_attn = n_heads * HEAD_DIM
    n_tiles = MOBA_TOPK * (MOBA_BLOCK // LANES)
    kern = functools.partial(_sample_attn_kernel, n_seq=bsz, n_heads=n_heads, past_len=past_len)
    col_spec = pl.BlockSpec((1, d_attn, 1), lambda b, sel, pt: (b, 0, 0))
    any_spec = pl.BlockSpec(memory_space=pl.ANY)
    grid_spec = pltpu.PrefetchScalarGridSpec(
        num_scalar_prefetch=2,
        grid=(bsz,),
        in_specs=[any_spec, any_spec, col_spec, col_spec, col_spec],
        out_specs=col_spec,
        scratch_shapes=[
            pltpu.VMEM((2, n_heads, n_tiles, HEAD_DIM, LANES), F32),
            pltpu.VMEM((2, n_heads, n_tiles, HEAD_DIM, LANES), F32),
            pltpu.SemaphoreType.DMA((2, 2)),
        ],
    )
    return pl.pallas_call(
        kern,
        grid_spec=grid_spec,
        out_shape=jax.ShapeDtypeStruct((bsz, d_attn, 1), F32),
        compiler_params=pltpu.CompilerParams(
            dimension_semantics=("arbitrary",), vmem_limit_bytes=32 * MIB),
        name="sample_moba_attention",
    )(sel_flat, page_table, kc, vc, qcol, kcol, vcol)


def _sample_ffn_kernel(x_ref, att_ref, cm_ref, wo_ref, g1_ref, b1_ref, wup_ref, wfc_ref, sf0_ref, sf1_ref,
                       wdn_ref, g2_ref, b2_ref, y_ref, up_ref, *, d_attn, d_ff, alpha):
    mix = (_dot(att_ref[...].astype(BF16), wo_ref[0:d_attn, :])
           + _dot(cm_ref[...].astype(BF16), wo_ref[d_attn:, :]))
    x1 = _layer_norm(alpha * x_ref[...] + mix, g1_ref[...], b1_ref[...])
    up = _dot(x1.astype(BF16), wup_ref[...])
    up_ref[...] = up
    w = wfc_ref[...]
    conv = w[0:1, :] * sf0_ref[...] + w[1:2, :] * sf1_ref[...] + w[2:3, :] * up
    hcol = (_silu(conv[:, 0:d_ff]) * conv[:, d_ff:]).astype(BF16)
    f = _dot(hcol, wdn_ref[...])
    y_ref[...] = _layer_norm(alpha * x1 + f, g2_ref[...], b2_ref[...])


def _sample_ffn(xs, att, cm, wo, g1, b1, wup, wfc, sf0, sf1, wdn, g2, b2, *, alpha):
    b, d_model = xs.shape
    d_ff = wdn.shape[0]
    kern = functools.partial(_sample_ffn_kernel, d_attn=att.shape[1], d_ff=d_ff, alpha=alpha)
    return pl.pallas_call(
        kern,
        out_shape=[jax.ShapeDtypeStruct((b, d_model), F32), jax.ShapeDtypeStruct((b, 2 * d_ff), F32)],
        compiler_params=pltpu.CompilerParams(vmem_limit_bytes=48 * MIB),
        name="sample_outproj_ffn",
    )(xs, att, cm, wo, g1, b1, wup, wfc, sf0, sf1, wdn, g2, b2)


def kernel(x_prompt, x_sample, cache_k, cache_v, state_conv, state_ffn_conv, page_table, w_in, w_conv, w_out,
           ln1_g, ln1_b, w_ffn_up, w_ffn_conv, w_ffn_down, ln2_g, ln2_b):
    depth = w_in.shape[0]
    assert depth == 1, "single-layer step"
    bsz, seq, d_model = x_prompt.shape
    dec_b, dec_t, _ = x_sample.shape
    assert dec_t == 1
    _, n_phys, page_size, n_heads, head_dim = cache_k.shape
    assert head_dim == HEAD_DIM and page_size == LANES
    d_attn = n_heads * head_dim
    d_conv = (w_in.shape[2] - 3 * d_attn) // 3
    d_ff = w_ffn_down.shape[1]
    past_len = page_table.shape[1] * page_size
    alpha = (2.0 * depth) ** 0.25
    row = lambda a: a.reshape(1, -1)

    w_in_b = w_in[0].astype(BF16)
    wt = w_in_b[:, :3 * d_attn].T
    wr = w_in_b[:, d_attn:2 * d_attn]
    wr = jnp.concatenate([wr, w_in_b[:, 3 * d_attn:]], axis=1)
    wo = w_out[0].astype(BF16)
    wup = w_ffn_up[0].astype(BF16)
    wdn = w_ffn_down[0].astype(BF16)
    wc, wfc = w_conv[0], w_ffn_conv[0]
    g1, b1, g2, b2 = row(ln1_g[0]), row(ln1_b[0]), row(ln2_g[0]), row(ln2_b[0])

    xs = x_sample.reshape(dec_b, d_model)
    st0, st1 = state_conv[0, :, 0, :], state_conv[0, :, 1, :]
    q_s, k_s, v_s, cm_s, u_s = _sample_inproj(xs, w_in[0], wc, st0, st1, d_attn=d_attn, d_conv=d_conv)
    kc = jnp.transpose(cache_k[0], (0, 2, 3, 1)).reshape(n_phys, d_attn, page_size)
    vc = jnp.transpose(cache_v[0], (0, 2, 3, 1)).reshape(n_phys, d_attn, page_size)
    qcol, kcol, vcol = q_s[:, :, None], k_s[:, :, None], v_s[:, :, None]

    x2d = x_prompt.reshape(bsz * seq, d_model)
    qt, kt, vt, krm, kmean, cm, conv_p = _prompt_inproj(x2d, wt, wr, wc, bsz=bsz, seq=seq, tm=512)
    att, sel = _prompt_attention_and_sample_gate(qt, krm, vt, kmean, page_table, kc, qcol, kcol,
                                                 bsz=bsz, seq=seq, past_len=past_len, pages_per_wave=16)
    y_p, ffn_p = _prompt_ffn(x2d, att, cm, wo, g1, b1, wup, wfc, wdn, g2, b2,
                             bsz=bsz, seq=seq, tm=512, cw=256, alpha=alpha)
    to_cache = lambda a: jnp.transpose(a.reshape(1, bsz, n_heads, head_dim, seq), (0, 1, 4, 2, 3))
    k_prompt, v_prompt = to_cache(kt), to_cache(vt)

    sel_flat = jnp.transpose(sel[:, :MOBA_TOPK, :n_heads], (0, 2, 1)).reshape(-1)
    att_s = _sample_attention(sel_flat, page_table, kc, vc, qcol, kcol, vcol,
                              n_heads=n_heads, past_len=past_len)
    sf0, sf1 = state_ffn_conv[0, :, 0, :], state_ffn_conv[0, :, 1, :]
    y_s, up_s = _sample_ffn(xs, att_s.reshape(dec_b, d_attn), cm_s, wo, g1, b1, wup, wfc, sf0, sf1,
                            wdn, g2, b2, alpha=alpha)

    return (y_p.reshape(bsz, seq, d_model),
            y_s.reshape(dec_b, 1, d_model),
            k_prompt, v_prompt,
            conv_p[None], ffn_p[None],
            k_s.reshape(1, dec_b, 1, n_heads, head_dim), v_s.reshape(1, dec_b, 1, n_heads, head_dim),
            jnp.stack([st1, u_s], axis=1)[None],
            jnp.stack([sf1, up_s], axis=1)[None])
```

```python
import functools

import jax
import jax.numpy as jnp
from jax import lax
from jax.experimental import pallas as pl
from jax.experimental.pallas import tpu as pltpu

F32 = jnp.float32
BF16 = jnp.bfloat16

HEAD_DIM = 64
MOBA_BLOCK = 256
MOBA_TOPK = 3
CONV_W = 3
LN_EPS = 1e-5
LOG2E = 1.4426950408889634
NEG = -1e30
SUBLANES = 8
LANES = 128
MIB = 1024 * 1024

_NT = (((1,), (1,)), ((), ()))


def _dot(a, b):
    return jnp.dot(a, b, preferred_element_type=F32)


def _dot_nt(a, b):
    return lax.dot_general(a, b, _NT, preferred_element_type=F32)


def _layer_norm(x, g, b):
    mu = jnp.mean(x, axis=-1, keepdims=True)
    xc = x - mu
    var = jnp.mean(xc * xc, axis=-1, keepdims=True)
    return xc * lax.rsqrt(var + LN_EPS) * g + b


def _silu(a):
    return a * jax.nn.sigmoid(a)


def _inproj_kernel(x_ref, wt_ref, wr_ref, wc_ref,
                   qt_ref, kt_ref, vt_ref, krm_ref, kmean_ref, cm_ref, cst_ref,
                   ubuf_ref, *, tm, tiles_per_batch, d_attn, d_conv):
    t = pl.program_id(0)
    tb = t % tiles_per_batch
    @pl.when(tb == 0)
    def _():
        ubuf_ref[0:SUBLANES, :] = jnp.zeros((SUBLANES, d_conv), F32)

    xb = x_ref[...].astype(BF16)

    gc = _dot(xb, wr_ref[:, d_attn + d_conv:d_attn + 2 * d_conv])
    hh = _dot(xb, wr_ref[:, d_attn + 2 * d_conv:d_attn + 3 * d_conv])
    u = gc * hh
    gb = _dot(xb, wr_ref[:, d_attn:d_attn + d_conv])
    ubuf_ref[SUBLANES:SUBLANES + tm, :] = u
    wc = wc_ref[...]
    conv = (wc[0:1, :] * ubuf_ref[SUBLANES - 2:SUBLANES - 2 + tm, :]
            + wc[1:2, :] * ubuf_ref[SUBLANES - 1:SUBLANES - 1 + tm, :]
            + wc[2:3, :] * u)
    cm_ref[...] = (gb * conv).astype(BF16)
    cst_ref[0] = ubuf_ref[SUBLANES + tm - (CONV_W - 1):SUBLANES + tm, :]
    ubuf_ref[0:SUBLANES, :] = ubuf_ref[tm:tm + SUBLANES, :]

    k = _dot(xb, wr_ref[:, 0:d_attn])
    kt_ref[0] = k.T
    krm_ref[...] = k.astype(BF16)
    for blk in range(tm // MOBA_BLOCK):
        ksum = jnp.sum(k[blk * MOBA_BLOCK:(blk + 1) * MOBA_BLOCK, :], axis=0, keepdims=True)
        kmean_ref[0, pl.ds(tb * (tm // MOBA_BLOCK) + blk, 1), :] = ksum * (1.0 / MOBA_BLOCK)

    qt_ref[0] = (_dot_nt(wt_ref[0:d_attn, :], xb) * (LOG2E / HEAD_DIM ** 0.5)).astype(BF16)
    vt_ref[0] = _dot_nt(wt_ref[2 * d_attn:3 * d_attn, :], xb)


def _prompt_inproj(x2d, wt, wr, w_conv, *, bsz, seq, tm):
    n, d_model = x2d.shape
    d_attn = wt.shape[0] // 3
    d_conv = (wr.shape[1] - d_attn) // 3
    tpb = seq // tm
    nb = seq // MOBA_BLOCK
    kern = functools.partial(_inproj_kernel, tm=tm, tiles_per_batch=tpb, d_attn=d_attn, d_conv=d_conv)
    const = lambda t: (0, 0)
    ct_spec = lambda: pl.BlockSpec((1, d_attn, tm), lambda t: (t // tpb, 0, t % tpb))
    return pl.pallas_call(
        kern,
        grid=(n // tm,),
        in_specs=[
            pl.BlockSpec((tm, d_model), lambda t: (t, 0)),
            pl.BlockSpec(wt.shape, const),
            pl.BlockSpec(wr.shape, const),
            pl.BlockSpec(w_conv.shape, const),
        ],
        out_specs=[
            ct_spec(), ct_spec(), ct_spec(),
            pl.BlockSpec((tm, d_attn), lambda t: (t, 0)),
            pl.BlockSpec((1, nb, d_attn), lambda t: (t // tpb, 0, 0)),
            pl.BlockSpec((tm, d_conv), lambda t: (t, 0)),
            pl.BlockSpec((1, CONV_W - 1, d_conv), lambda t: (t // tpb, 0, 0)),
        ],
        out_shape=[
            jax.ShapeDtypeStruct((bsz, d_attn, seq), BF16),
            jax.ShapeDtypeStruct((bsz, d_attn, seq), F32),
            jax.ShapeDtypeStruct((bsz, d_attn, seq), F32),
            jax.ShapeDtypeStruct((n, d_attn), BF16),
            jax.ShapeDtypeStruct((bsz, nb, d_attn), F32),
            jax.ShapeDtypeStruct((n, d_conv), BF16),
            jax.ShapeDtypeStruct((bsz, CONV_W - 1, d_conv), F32),
        ],
        scratch_shapes=[pltpu.VMEM((tm + 2 * SUBLANES, d_conv), F32)],
        compiler_params=pltpu.CompilerParams(
            dimension_semantics=("arbitrary",), vmem_limit_bytes=56 * MIB),
        name="prompt_inproj",
    )(x2d, wt, wr, w_conv)


BIAS_CH0 = 16
GATE_PREFETCH = 2
DOWN_LAG = 7
DENOM_ROWS = 16


def _key_bias_channels(seq):
    pos = jnp.arange(seq, dtype=jnp.int32)[:, None]
    c = jnp.arange(2 * HEAD_DIM, dtype=jnp.int32)[None, :] - BIAS_CH0
    kblk, koff = pos // MOBA_BLOCK, pos % MOBA_BLOCK
    ext = jnp.where(c + BIAS_CH0 < SUBLANES, (kblk == c + BIAS_CH0).astype(F32), 0.0)
    ext = jnp.where((c >= 0) & (c < 3), koff.astype(F32), ext)
    ext = jnp.where((c >= 3) & (c < 6), (kblk * MOBA_BLOCK).astype(F32), ext)
    ext = jnp.where((c >= 6) & (c < 12), 1.0, ext)
    return ext.astype(BF16)


def _query_alibi_channels(coef, block, nq):
    row = lax.broadcasted_iota(jnp.int32, (2 * SUBLANES, nq), 0)
    qoff = lax.broadcasted_iota(jnp.int32, (2 * SUBLANES, nq), 1).astype(F32)
    cb = jnp.full((2 * SUBLANES, nq), 1.0, F32) * coef
    val = jnp.where(row < 6, cb,
                    jnp.where(row < 9, -cb * qoff,
                              jnp.where(row < 12, -cb * float(block * MOBA_BLOCK), 0.0)))
    hi = val.astype(BF16).astype(F32)
    mid = (val - hi).astype(BF16).astype(F32)
    lo = ((val - hi) - mid).astype(BF16).astype(F32)
    part = row % 3
    return jnp.where(part == 0, hi, jnp.where(part == 1, mid, lo))


def _selection_bias(g, n_past):
    sub = lax.broadcasted_iota(jnp.int32, g.shape, 0)
    rank = jnp.zeros(g.shape, jnp.int32)
    for jp in range(n_past):
        row = g[jp:jp + 1, :]
        beats = (row > g) | ((row == g) & (jp < sub))
        rank = rank + jnp.where(beats, 1, 0)
    sel = ((sub < n_past) & (rank < MOBA_TOPK)) | (sub == n_past)
    return jnp.where(sel, 0.0, NEG).astype(F32)


def _gate_wave_copies(pt_ref, kc_ref, buf_ref, sem_ref, wave, slot, *, pages_per_wave, pages_per_seq):
    first = wave * pages_per_wave
    seq_idx = first // pages_per_seq
    pg0 = first % pages_per_seq
    return [pltpu.make_async_copy(kc_ref.at[pt_ref[seq_idx, pg0 + pg]], buf_ref.at[slot, pg], sem_ref.at[slot])
            for pg in range(pages_per_wave)]


def _sample_gate_wave(copies, wave, w, part, n_parts, n_waves, qcol_ref, buf_ref, g_ref,
                      *, pages_per_wave, pages_per_seq, n_heads):
    slot = w
    n_slots = buf_ref.shape[0]
    pages_per_block = MOBA_BLOCK // LANES
    blocks_per_part = pages_per_wave // pages_per_block // n_parts
    n_blocks = pages_per_seq // pages_per_block
    row0 = (wave * pages_per_wave % pages_per_seq) // pages_per_block

    if part == 0:
        for cp in copies(jnp.minimum(wave + GATE_PREFETCH, n_waves - 1), (w + GATE_PREFETCH) % n_slots):
            cp.start()
        for cp in copies(wave, slot):
            cp.wait()
        g_ref[:, n_blocks:, :] = jnp.zeros((n_heads, g_ref.shape[1] - n_blocks, LANES), F32)

    qcol = qcol_ref[0]
    for h in range(n_heads):
        qb = jnp.broadcast_to(qcol[h * HEAD_DIM:(h + 1) * HEAD_DIM, :], (HEAD_DIM, LANES))
        for bl in range(part * blocks_per_part, (part + 1) * blocks_per_part):
            ksum = None
            for pg in range(pages_per_block):
                kt = buf_ref[slot, bl * pages_per_block + pg, h * HEAD_DIM:(h + 1) * HEAD_DIM, :]
                ksum = kt if ksum is None else ksum + kt
            g_ref[h, pl.ds(row0 + bl, 1), :] = jnp.sum(qb * ksum, axis=0, keepdims=True)


def _sample_gate_topk(qcol_ref, kcol_ref, sel_ref, g_ref, *, n_heads, past_len):
    qcol = qcol_ref[0]
    kcol = kcol_ref[0]
    own = past_len // MOBA_BLOCK
    nrow = g_ref.shape[1]
    lane = lax.broadcasted_iota(jnp.int32, (nrow, LANES), 1)
    sub = lax.broadcasted_iota(jnp.int32, (nrow, LANES), 0)
    gate = jnp.full((nrow, LANES), -jnp.inf, F32)
    for h in range(n_heads):
        tot = jnp.sum(g_ref[h], axis=1, keepdims=True)
        own_dot = jnp.sum(qcol[h * HEAD_DIM:(h + 1) * HEAD_DIM, :] * kcol[h * HEAD_DIM:(h + 1) * HEAD_DIM, :],
                          axis=0, keepdims=True)
        tot = jnp.where(sub[:, 0:1] == own, own_dot, tot) * (1.0 / MOBA_BLOCK)
        gate = jnp.where(lane == h, tot, gate)
    gate = jnp.where(sub < own, gate, -jnp.inf)
    rows = []
    for _ in range(MOBA_TOPK):
        m = jnp.max(gate, axis=0, keepdims=True)
        idx = jnp.min(jnp.where(gate == m, sub, nrow), axis=0, keepdims=True)
        rows.append(idx)
        gate = jnp.where(sub == idx, -jnp.inf, gate)
    rows.append(jnp.zeros((SUBLANES - MOBA_TOPK, LANES), jnp.int32))
    sel_ref[0] = jnp.concatenate(rows, axis=0)


def _attn_kernel(pt_ref, qt_ref, k_ref, kx_ref, vt_ref, kmean_ref, kc_ref, qcol_ref, kcol_ref,
                 o_ref, sel_ref,
                 kk_ref, vtb_ref, pbuf_ref, g_ref, sem_ref,
                 *, nb, n_grid, waves_per_step, pages_per_wave, pages_per_seq, n_heads, past_len):
    pair = pl.program_id(1)
    step = pl.program_id(0) * pl.num_programs(1) + pair
    blk = MOBA_BLOCK
    pw = 2 * HEAD_DIM

    n_waves = n_grid * waves_per_step
    copies = functools.partial(_gate_wave_copies, pt_ref, kc_ref, pbuf_ref, sem_ref,
                               pages_per_wave=pages_per_wave, pages_per_seq=pages_per_seq)
    gate_wave = functools.partial(_sample_gate_wave, copies, qcol_ref=qcol_ref, buf_ref=pbuf_ref, g_ref=g_ref,
                                  n_waves=n_waves, pages_per_wave=pages_per_wave, pages_per_seq=pages_per_seq,
                                  n_heads=n_heads)

    @pl.when(step == 0)
    def _():
        for ahead in range(GATE_PREFETCH):
            for cp in copies(ahead, ahead):
                cp.start()

    kk_ref[:, 0:pw] = k_ref[...]
    kk_ref[:, pw:2 * pw] = kx_ref[...]
    vrows = HEAD_DIM + DENOM_ROWS
    ones_row = jnp.where(lax.broadcasted_iota(jnp.int32, (DENOM_ROWS, vt_ref.shape[2]), 0) == 0,
                         1.0, 0.0).astype(BF16)
    for e in range(2):
        vtb_ref[e * vrows:e * vrows + HEAD_DIM, :] = vt_ref[0, e * HEAD_DIM:(e + 1) * HEAD_DIM, :].astype(BF16)
        vtb_ref[e * vrows + HEAD_DIM:(e + 1) * vrows, :] = ones_row

    slope_even = jnp.where(pair == 0, 2.0 ** -1,
                           jnp.where(pair == 1, 2.0 ** -3,
                                     jnp.where(pair == 2, 2.0 ** -5, 2.0 ** -7))).astype(F32)
    causal = (lax.broadcasted_iota(jnp.int32, (blk, blk), 1)
              >= lax.broadcasted_iota(jnp.int32, (blk, blk), 0))
    zeros8 = jnp.zeros((SUBLANES, blk), F32)
    zeros_h = jnp.zeros((HEAD_DIM, blk), BF16)
    zeros_tail = jnp.zeros((pw - 2 * BIAS_CH0, blk), BF16)
    kmean = kmean_ref[0]

    def scores(i, e):
        sl = slope_even * (0.5 ** e)
        qh = qt_ref[0, e * HEAD_DIM:(e + 1) * HEAD_DIM, i * blk:(i + 1) * blk]
        if i > 0:
            gate = jnp.dot(kmean[:, e * HEAD_DIM:(e + 1) * HEAD_DIM], qh.astype(F32),
                           preferred_element_type=F32, precision=lax.Precision.HIGHEST)
        else:
            gate = zeros8
        selb = _selection_bias(gate, i)
        q_ext = jnp.concatenate([qh, zeros_h] if e == 0 else [zeros_h, qh], axis=0)
        q_ext = jnp.concatenate(
            [q_ext,
             jnp.concatenate([selb, zeros8], axis=0).astype(BF16),
             _query_alibi_channels(sl * LOG2E, i, blk).astype(BF16),
             zeros_tail], axis=0)
        s = _dot(kk_ref[0:(i + 1) * blk, :], q_ext)
        s_own = jnp.where(causal, s[i * blk:, :], NEG)
        return s_own if i == 0 else jnp.concatenate([s[0:i * blk, :], s_own], axis=0)

    def softmax(s):
        m = jnp.max(s, axis=0, keepdims=True)
        return (jnp.exp2(s - m).astype(BF16),)

    def weighted_values(i, e, pr):
        acc = _dot(vtb_ref[e * vrows:(e + 1) * vrows, 0:(i + 1) * blk], pr)
        return acc[0:HEAD_DIM, :] * (1.0 / acc[HEAD_DIM:HEAD_DIM + 1, :])

    order = [(i, e) for i in range(nb) for e in range(2)]
    n_steps = len(order)
    s_buf, p_buf, outs = {}, {}, []
    parts_per_wave = n_steps // waves_per_step
    work = [i + 1 for i, _ in order]
    gate_parts_at = {n: [] for n in range(n_steps)}
    for k in range(n_steps):
        target = (k + 0.5) * sum(work) / n_steps
        gate_parts_at[next(n for n in range(n_steps) if sum(work[:n + 1]) >= target)].append(k)
    for n in range(-1, n_steps + 1):
        for k in gate_parts_at.get(n + 1, ()):
            w, part = divmod(k, parts_per_wave)
            gate_wave(step * waves_per_step + w, w, part, parts_per_wave)
        if 0 <= n + 1 < n_steps:
            s_buf[n + 1] = scores(*order[n + 1])
        if 0 <= n < n_steps:
            p_buf[n] = softmax(s_buf.pop(n))
        if n >= 1:
            i, e = order[n - 1]
            outs.append(weighted_values(i, e, *p_buf.pop(n - 1)))
            if e == 1:
                o_ref[i * blk:(i + 1) * blk, :] = jnp.concatenate(outs, axis=0).T.astype(BF16)
                outs = []

    @pl.when((step + 1) * waves_per_step * pages_per_wave % pages_per_seq == 0)
    def _():
        _sample_gate_topk(qcol_ref, kcol_ref, sel_ref, g_ref, n_heads=n_heads, past_len=past_len)

    @pl.when(step == n_grid - 1)
    def _():
        for ahead in range(GATE_PREFETCH):
            for cp in copies(n_waves - 1, ahead):
                cp.wait()


def _prompt_attention_and_sample_gate(qt, krm, vt, kmean, page_table, kc, qcol, kcol,
                                      *, bsz, seq, past_len, pages_per_wave):
    d_attn = qt.shape[1]
    nb = seq // MOBA_BLOCK
    assert nb <= SUBLANES, "selection-bias channels hold one block per row of a vector register"
    n_pairs = d_attn // (2 * HEAD_DIM)
    n_heads = d_attn // HEAD_DIM
    pw = 2 * HEAD_DIM
    n_grid = bsz * n_pairs
    n_seq, pages_per_seq = page_table.shape
    pages_per_step = n_seq * pages_per_seq // n_grid
    assert pages_per_step * n_grid == n_seq * pages_per_seq and pages_per_seq % pages_per_step == 0
    waves_per_step = pages_per_step // pages_per_wave
    assert waves_per_step * pages_per_wave == pages_per_step and waves_per_step > GATE_PREFETCH
    assert (2 * nb) % waves_per_step == 0
    assert pages_per_wave % ((MOBA_BLOCK // LANES) * (2 * nb // waves_per_step)) == 0
    gate_rows = ((pages_per_seq * LANES // MOBA_BLOCK + 1 + SUBLANES - 1) // SUBLANES) * SUBLANES
    seq_of = lambda b, p: (b * n_pairs + p) * pages_per_step // pages_per_seq
    col_spec = pl.BlockSpec((1, d_attn, 1), lambda b, p, pt: (seq_of(b, p), 0, 0))
    kern = functools.partial(_attn_kernel, nb=nb, n_grid=n_grid, waves_per_step=waves_per_step,
                             pages_per_wave=pages_per_wave, pages_per_seq=pages_per_seq, n_heads=n_heads,
                             past_len=past_len)
    grid_spec = pltpu.PrefetchScalarGridSpec(
        num_scalar_prefetch=1,
        grid=(bsz, n_pairs),
        in_specs=[
            pl.BlockSpec((1, pw, seq), lambda b, p, pt: (b, p, 0)),
            pl.BlockSpec((seq, pw), lambda b, p, pt: (b, p)),
            pl.BlockSpec((seq, pw), lambda b, p, pt: (0, 0)),
            pl.BlockSpec((1, pw, seq), lambda b, p, pt: (b, p, 0)),
            pl.BlockSpec((1, nb, pw), lambda b, p, pt: (b, 0, p)),
            pl.BlockSpec(memory_space=pl.ANY),
            col_spec, col_spec,
        ],
        out_specs=[
            pl.BlockSpec((seq, pw), lambda b, p, pt: (b, p)),
            pl.BlockSpec((1, SUBLANES, LANES), lambda b, p, pt: (seq_of(b, p), 0, 0)),
        ],
        scratch_shapes=[
            pltpu.VMEM((seq, 2 * pw), BF16),
            pltpu.VMEM((pw + 2 * DENOM_ROWS, seq), BF16),
            pltpu.VMEM((waves_per_step, pages_per_wave, d_attn, LANES), F32),
            pltpu.VMEM((n_heads, gate_rows, LANES), F32),
            pltpu.SemaphoreType.DMA((waves_per_step,)),
        ],
    )
    return pl.pallas_call(
        kern,
        grid_spec=grid_spec,
        out_shape=[jax.ShapeDtypeStruct((bsz * seq, d_attn), BF16),
                   jax.ShapeDtypeStruct((n_seq, SUBLANES, LANES), jnp.int32)],
        compiler_params=pltpu.CompilerParams(
            dimension_semantics=("arbitrary", "arbitrary"), vmem_limit_bytes=48 * MIB),
        name="prompt_moba_attention",
    )(page_table, qt, krm, _key_bias_channels(seq), vt, kmean, kc, qcol, kcol)


def _ffn_kernel(x_ref, att_ref, cm_ref, wo_ref, g1_ref, b1_ref, wup_ref, wfc_ref, wdn_ref, g2_ref, b2_ref,
                y_ref, fst_ref,
                x1_ref, x1b_ref, acc_ref, h_ref, ua_ref, ug_ref, carry_ref,
                *, tm, tiles_per_batch, d_attn, d_ff, cw, alpha):
    t = pl.program_id(0)
    mix = _dot(att_ref[...], wo_ref[0:d_attn, :]) + _dot(cm_ref[...], wo_ref[d_attn:, :])
    x1 = _layer_norm(alpha * x_ref[...] + mix, g1_ref[...], b1_ref[...])
    x1_ref[...] = x1
    x1b_ref[...] = x1.astype(BF16)

    @pl.when(t % tiles_per_batch == 0)
    def _():
        carry_ref[...] = jnp.zeros(carry_ref.shape, F32)

    def up_project(col0, ubuf):
        ubuf[0:SUBLANES, :] = carry_ref[:, col0:col0 + cw]
        ubuf[SUBLANES:SUBLANES + tm, :] = _dot(x1b_ref[...], wup_ref[:, col0:col0 + cw])

    def token_conv(col0, ubuf):
        w = wfc_ref[:, col0:col0 + cw]
        conv = (w[0:1, :] * ubuf[SUBLANES - 2:SUBLANES - 2 + tm, :]
                + w[1:2, :] * ubuf[SUBLANES - 1:SUBLANES - 1 + tm, :]
                + w[2:3, :] * ubuf[SUBLANES:SUBLANES + tm, :])
        carry_ref[:, col0:col0 + cw] = ubuf[tm:tm + SUBLANES, :]
        fst_ref[0, :, col0:col0 + cw] = ubuf[SUBLANES + tm - (CONV_W - 1):SUBLANES + tm, :]
        return conv

    n_chunks = d_ff // cw
    for c in range(-1, n_chunks + DOWN_LAG):
        if 0 <= c + 1 < n_chunks:
            up_project((c + 1) * cw, ua_ref.at[(c + 1) % 2])
        if 0 <= c < n_chunks:
            a = _silu(token_conv(c * cw, ua_ref.at[c % 2]))
        if 0 <= c + 1 < n_chunks:
            up_project(d_ff + (c + 1) * cw, ug_ref.at[(c + 1) % 2])
        if 0 <= c < n_chunks:
            h_ref[:, c * cw:(c + 1) * cw] = (a * token_conv(d_ff + c * cw, ug_ref.at[c % 2])).astype(BF16)
        if c >= DOWN_LAG:
            k = c - DOWN_LAG
            part = _dot(h_ref[:, k * cw:(k + 1) * cw], wdn_ref[k * cw:(k + 1) * cw, :])
            if k == 0:
                acc_ref[...] = part
            else:
                acc_ref[...] += part

    y_ref[...] = _layer_norm(alpha * x1_ref[...] + acc_ref[...], g2_ref[...], b2_ref[...])


def _prompt_ffn(x2d, att, cm, wo, g1, b1, wup, wfc, wdn, g2, b2, *, bsz, seq, tm, cw, alpha):
    n, d_model = x2d.shape
    d_attn = att.shape[1]
    d_ff = wdn.shape[0]
    tpb = seq // tm
    kern = functools.partial(_ffn_kernel, tm=tm, tiles_per_batch=tpb, d_attn=d_attn, d_ff=d_ff, cw=cw,
                             alpha=alpha)
    const = lambda t: (0, 0)
    resident = lambda a: pl.BlockSpec(a.shape, const, pipeline_mode=pl.Buffered(1))
    row = lambda w: pl.BlockSpec((tm, w), lambda t: (t, 0))
    return pl.pallas_call(
        kern,
        grid=(n // tm,),
        in_specs=[row(d_model), row(d_attn), row(cm.shape[1]),
                  resident(wo), resident(g1), resident(b1), resident(wup), resident(wfc), resident(wdn),
                  resident(g2), resident(b2)],
        out_specs=[row(d_model),
                   pl.BlockSpec((1, CONV_W - 1, 2 * d_ff), lambda t: (t // tpb, 0, 0))],
        out_shape=[jax.ShapeDtypeStruct((n, d_model), F32),
                   jax.ShapeDtypeStruct((bsz, CONV_W - 1, 2 * d_ff), F32)],
        scratch_shapes=[
            pltpu.VMEM((tm, d_model), F32),
            pltpu.VMEM((tm, d_model), BF16),
            pltpu.VMEM((tm, d_model), F32),
            pltpu.VMEM((tm, d_ff), BF16),
            pltpu.VMEM((2, tm + 2 * SUBLANES, cw), F32),
            pltpu.VMEM((2, tm + 2 * SUBLANES, cw), F32),
            pltpu.VMEM((SUBLANES, 2 * d_ff), F32),
        ],
        compiler_params=pltpu.CompilerParams(
            dimension_semantics=("arbitrary",), vmem_limit_bytes=56 * MIB),
        name="prompt_outproj_ffn",
    )(x2d, att, cm, wo, g1, b1, wup, wfc, wdn, g2, b2)


def _sample_inproj_kernel(x_ref, w_ref, wc_ref, st0_ref, st1_ref,
                          q_ref, k_ref, v_ref, cm_ref, u_ref, *, d_attn, d_conv):
    z = jnp.dot(x_ref[...], w_ref[...], preferred_element_type=F32,
                precision=lax.Precision.HIGHEST)
    q_ref[...] = z[:, 0:d_attn] * 0.125
    k_ref[...] = z[:, d_attn:2 * d_attn]
    v_ref[...] = z[:, 2 * d_attn:3 * d_attn]
    o = 3 * d_attn
    gb = z[:, o:o + d_conv]
    u = z[:, o + d_conv:o + 2 * d_conv] * z[:, o + 2 * d_conv:o + 3 * d_conv]
    wc = wc_ref[...]
    conv = wc[0:1, :] * st0_ref[...] + wc[1:2, :] * st1_ref[...] + wc[2:3, :] * u
    cm_ref[...] = gb * conv
    u_ref[...] = u


def _sample_inproj(xs, w_in_f32, w_conv, st0, st1, *, d_attn, d_conv):
    b = xs.shape[0]
    kern = functools.partial(_sample_inproj_kernel, d_attn=d_attn, d_conv=d_conv)
    sd = lambda w: jax.ShapeDtypeStruct((b, w), F32)
    return pl.pallas_call(
        kern,
        out_shape=[sd(d_attn), sd(d_attn), sd(d_attn), sd(d_conv), sd(d_conv)],
        compiler_params=pltpu.CompilerParams(vmem_limit_bytes=48 * MIB),
        name="sample_inproj",
    )(xs, w_in_f32, w_conv, st0, st1)


def _sel_copies(sel_ref, pt_ref, kc_ref, vc_ref, kbuf_ref, vbuf_ref, sem_ref, b, slot, *, n_heads):
    pages_per_block = MOBA_BLOCK // LANES
    copies = []
    for h in range(n_heads):
        for r in range(MOBA_TOPK):
            blk = sel_ref[(b * n_heads + h) * MOBA_TOPK + r]
            for pg in range(pages_per_block):
                phys = pt_ref[b, blk * pages_per_block + pg]
                rows = pl.ds(h * HEAD_DIM, HEAD_DIM)
                t = r * pages_per_block + pg
                copies.append(pltpu.make_async_copy(kc_ref.at[phys, rows], kbuf_ref.at[slot, h, t],
                                                    sem_ref.at[0, slot]))
                copies.append(pltpu.make_async_copy(vc_ref.at[phys, rows], vbuf_ref.at[slot, h, t],
                                                    sem_ref.at[1, slot]))
    return copies


def _sample_attn_kernel(sel_ref, pt_ref, kc_ref, vc_ref, qcol_ref, kcol_ref, vcol_ref, o_ref,
                        kbuf_ref, vbuf_ref, sem_ref, *, n_seq, n_heads, past_len):
    b = pl.program_id(0)
    slot = b % 2
    pages_per_block = MOBA_BLOCK // LANES
    n_tiles = MOBA_TOPK * pages_per_block
    copies = functools.partial(_sel_copies, sel_ref, pt_ref, kc_ref, vc_ref, kbuf_ref, vbuf_ref, sem_ref,
                               n_heads=n_heads)

    @pl.when(b == 0)
    def _():
        for cp in copies(b, slot):
            cp.start()

    @pl.when(b + 1 < n_seq)
    def _():
        for cp in copies(b + 1, 1 - slot):
            cp.start()

    for cp in copies(b, slot):
        cp.wait()

    qcol = qcol_ref[0]
    kcol = kcol_ref[0]
    vcol = vcol_ref[0]
    lane = lax.broadcasted_iota(jnp.int32, (1, LANES), 1)
    heads = range(n_heads)
    hs = [slice(h * HEAD_DIM, (h + 1) * HEAD_DIM) for h in heads]

    s_own, s_all = [], []
    for h in heads:
        qb = jnp.broadcast_to(qcol[hs[h], :], (HEAD_DIM, LANES))
        s_own.append(jnp.sum(qcol[hs[h], :] * kcol[hs[h], :], axis=0, keepdims=True))
        rows = []
        for t in range(n_tiles):
            blk = sel_ref[(b * n_heads + h) * MOBA_TOPK + t // pages_per_block]
            pos0 = blk * MOBA_BLOCK + (t % pages_per_block) * LANES
            dist = (past_len - pos0 - lane).astype(F32)
            rows.append(jnp.sum(qb * kbuf_ref[slot, h, t], axis=0, keepdims=True) - 2.0 ** -(h + 1) * dist)
        s_all.append(jnp.concatenate(rows, axis=0))
    m = [jnp.maximum(s_own[h], jnp.max(s_all[h], axis=(0, 1), keepdims=True)) for h in heads]
    pr = [jnp.exp(s_all[h] - m[h]) for h in heads]
    p_own = [jnp.exp(s_own[h] - m[h]) for h in heads]
    l = [p_own[h] + jnp.sum(pr[h], axis=(0, 1), keepdims=True) for h in heads]
    for h in heads:
        acc = p_own[h] * vcol[hs[h], :]
        wv = None
        for t in range(n_tiles):
            part = vbuf_ref[slot, h, t] * pr[h][t:t + 1, :]
            wv = part if wv is None else wv + part
        acc = acc + jnp.sum(wv, axis=1, keepdims=True)
        o_ref[0, hs[h], :] = acc * (1.0 / l[h])


def _sample_attention(sel_flat, page_table, kc, vc, qcol, kcol, vcol, *, n_heads, past_len):
    bsz = page_table.shape[0]
    d_attn = n_heads * HEAD_DIM
    n_tiles = MOBA_TOPK * (MOBA_BLOCK // LANES)
    kern = functools.partial(_sample_attn_kernel, n_seq=bsz, n_heads=n_heads, past_len=past_len)
    col_spec = pl.BlockSpec((1, d_attn, 1), lambda b, sel, pt: (b, 0, 0))
    any_spec = pl.BlockSpec(memory_space=pl.ANY)
    grid_spec = pltpu.PrefetchScalarGridSpec(
        num_scalar_prefetch=2,
        grid=(bsz,),
        in_specs=[any_spec, any_spec, col_spec, col_spec, col_spec],
        out_specs=col_spec,
        scratch_shapes=[
            pltpu.VMEM((2, n_heads, n_tiles, HEAD_DIM, LANES), F32),
            pltpu.VMEM((2, n_heads, n_tiles, HEAD_DIM, LANES), F32),
            pltpu.SemaphoreType.DMA((2, 2)),
        ],
    )
    return pl.pallas_call(
        kern,
        grid_spec=grid_spec,
        out_shape=jax.ShapeDtypeStruct((bsz, d_attn, 1), F32),
        compiler_params=pltpu.CompilerParams(
            dimension_semantics=("arbitrary",), vmem_limit_bytes=32 * MIB),
        name="sample_moba_attention",
    )(sel_flat, page_table, kc, vc, qcol, kcol, vcol)


def _sample_ffn_kernel(x_ref, att_ref, cm_ref, wo_ref, g1_ref, b1_ref, wup_ref, wfc_ref, sf0_ref, sf1_ref,
                       wdn_ref, g2_ref, b2_ref, y_ref, up_ref, *, d_attn, d_ff, alpha):
    mix = (_dot(att_ref[...].astype(BF16), wo_ref[0:d_attn, :])
           + _dot(cm_ref[...].astype(BF16), wo_ref[d_attn:, :]))
    x1 = _layer_norm(alpha * x_ref[...] + mix, g1_ref[...], b1_ref[...])
    up = _dot(x1.astype(BF16), wup_ref[...])
    up_ref[...] = up
    w = wfc_ref[...]
    conv = w[0:1, :] * sf0_ref[...] + w[1:2, :] * sf1_ref[...] + w[2:3, :] * up
    hcol = (_silu(conv[:, 0:d_ff]) * conv[:, d_ff:]).astype(BF16)
    f = _dot(hcol, wdn_ref[...])
    y_ref[...] = _layer_norm(alpha * x1 + f, g2_ref[...], b2_ref[...])


def _sample_ffn(xs, att, cm, wo, g1, b1, wup, wfc, sf0, sf1, wdn, g2, b2, *, alpha):
    b, d_model = xs.shape
    d_ff = wdn.shape[0]
    kern = functools.partial(_sample_ffn_kernel, d_attn=att.shape[1], d_ff=d_ff, alpha=alpha)
    return pl.pallas_call(
        kern,
        out_shape=[jax.ShapeDtypeStruct((b, d_model), F32), jax.ShapeDtypeStruct((b, 2 * d_ff), F32)],
        compiler_params=pltpu.CompilerParams(vmem_limit_bytes=48 * MIB),
        name="sample_outproj_ffn",
    )(xs, att, cm, wo, g1, b1, wup, wfc, sf0, sf1, wdn, g2, b2)


def kernel(x_prompt, x_sample, cache_k, cache_v, state_conv, state_ffn_conv, page_table, w_in, w_conv, w_out,
           ln1_g, ln1_b, w_ffn_up, w_ffn_conv, w_ffn_down, ln2_g, ln2_b):
    depth = w_in.shape[0]
    assert depth == 1, "single-layer step"
    bsz, seq, d_model = x_prompt.shape
    dec_b, dec_t, _ = x_sample.shape
    assert dec_t == 1
    _, n_phys, page_size, n_heads, head_dim = cache_k.shape
    assert head_dim == HEAD_DIM and page_size == LANES
    d_attn = n_heads * head_dim
    d_conv = (w_in.shape[2] - 3 * d_attn) // 3
    d_ff = w_ffn_down.shape[1]
    past_len = page_table.shape[1] * page_size
    alpha = (2.0 * depth) ** 0.25
    row = lambda a: a.reshape(1, -1)

    w_in_b = w_in[0].astype(BF16)
    wt = w_in_b[:, :3 * d_attn].T
    wr = w_in_b[:, d_attn:2 * d_attn]
    wr = jnp.concatenate([wr, w_in_b[:, 3 * d_attn:]], axis=1)
    wo = w_out[0].astype(BF16)
    wup = w_ffn_up[0].astype(BF16)
    wdn = w_ffn_down[0].astype(BF16)
    wc, wfc = w_conv[0], w_ffn_conv[0]
    g1, b1, g2, b2 = row(ln1_g[0]), row(ln1_b[0]), row(ln2_g[0]), row(ln2_b[0])

    xs = x_sample.reshape(dec_b, d_model)
    st0, st1 = state_conv[0, :, 0, :], state_conv[0, :, 1, :]
    q_s, k_s, v_s, cm_s, u_s = _sample_inproj(xs, w_in[0], wc, st0, st1, d_attn=d_attn, d_conv=d_conv)
    kc = jnp.transpose(cache_k[0], (0, 2, 3, 1)).reshape(n_phys, d_attn, page_size)
    vc = jnp.transpose(cache_v[0], (0, 2, 3, 1)).reshape(n_phys, d_attn, page_size)
    qcol, kcol, vcol = q_s[:, :, None], k_s[:, :, None], v_s[:, :, None]

    x2d = x_prompt.reshape(bsz * seq, d_model)
    qt, kt, vt, krm, kmean, cm, conv_p = _prompt_inproj(x2d, wt, wr, wc, bsz=bsz, seq=seq, tm=1024)
    att, sel = _prompt_attention_and_sample_gate(qt, krm, vt, kmean, page_table, kc, qcol, kcol,
                                                 bsz=bsz, seq=seq, past_len=past_len, pages_per_wave=16)
    y_p, ffn_p = _prompt_ffn(x2d, att, cm, wo, g1, b1, wup, wfc, wdn, g2, b2,
                             bsz=bsz, seq=seq, tm=512, cw=256, alpha=alpha)
    to_cache = lambda a: jnp.transpose(a.reshape(1, bsz, n_heads, head_dim, seq), (0, 1, 4, 2, 3))
    k_prompt, v_prompt = to_cache(kt), to_cache(vt)

    sel_flat = jnp.transpose(sel[:, :MOBA_TOPK, :n_heads], (0, 2, 1)).reshape(-1)
    att_s = _sample_attention(sel_flat, page_table, kc, vc, qcol, kcol, vcol,
                              n_heads=n_heads, past_len=past_len)
    sf0, sf1 = state_ffn_conv[0, :, 0, :], state_ffn_conv[0, :, 1, :]
    y_s, up_s = _sample_ffn(xs, att_s.reshape(dec_b, d_attn), cm_s, wo, g1, b1, wup, wfc, sf0, sf1,
                            wdn, g2, b2, alpha=alpha)

    return (y_p.reshape(bsz, seq, d_model),
            y_s.reshape(dec_b, 1, d_model),
            k_prompt, v_prompt,
            conv_p[None], ffn_p[None],
            k_s.reshape(1, dec_b, 1, n_heads, head_dim), v_s.reshape(1, dec_b, 1, n_heads, head_dim),
            jnp.stack([st1, u_s], axis=1)[None],
            jnp.stack([sf1, up_s], axis=1)[None])
```

```python
import functools

import jax
import jax.numpy as jnp
from jax import lax
from jax.experimental import pallas as pl
from jax.experimental.pallas import tpu as pltpu

F32 = jnp.float32
BF16 = jnp.bfloat16

HEAD_DIM = 64
MOBA_BLOCK = 256
MOBA_TOPK = 3
CONV_W = 3
LN_EPS = 1e-5
LOG2E = 1.4426950408889634
NEG = -1e30
SUBLANES = 8
LANES = 128
MIB = 1024 * 1024

_NT = (((1,), (1,)), ((), ()))


def _dot(a, b):
    return jnp.dot(a, b, preferred_element_type=F32)


def _dot_nt(a, b):
    return lax.dot_general(a, b, _NT, preferred_element_type=F32)


def _layer_norm(x, g, b):
    mu = jnp.mean(x, axis=-1, keepdims=True)
    xc = x - mu
    var = jnp.mean(xc * xc, axis=-1, keepdims=True)
    return xc * lax.rsqrt(var + LN_EPS) * g + b


def _silu(a):
    return a * jax.nn.sigmoid(a)


def _inproj_kernel(x_ref, wt_ref, wr_ref, wc_ref,
                   qt_ref, kt_ref, vt_ref, krm_ref, kmean_ref, cm_ref, cst_ref,
                   ubuf_ref, *, tm, tiles_per_batch, d_attn, d_conv):
    t = pl.program_id(0)
    tb = t % tiles_per_batch
    @pl.when(tb == 0)
    def _():
        ubuf_ref[0:SUBLANES, :] = jnp.zeros((SUBLANES, d_conv), F32)

    xb = x_ref[...].astype(BF16)

    gc = _dot(xb, wr_ref[:, d_attn + d_conv:d_attn + 2 * d_conv])
    hh = _dot(xb, wr_ref[:, d_attn + 2 * d_conv:d_attn + 3 * d_conv])
    u = gc * hh
    gb = _dot(xb, wr_ref[:, d_attn:d_attn + d_conv])
    ubuf_ref[SUBLANES:SUBLANES + tm, :] = u
    wc = wc_ref[...]
    conv = (wc[0:1, :] * ubuf_ref[SUBLANES - 2:SUBLANES - 2 + tm, :]
            + wc[1:2, :] * ubuf_ref[SUBLANES - 1:SUBLANES - 1 + tm, :]
            + wc[2:3, :] * u)
    cm_ref[...] = (gb * conv).astype(BF16)
    cst_ref[0] = ubuf_ref[SUBLANES + tm - (CONV_W - 1):SUBLANES + tm, :]
    ubuf_ref[0:SUBLANES, :] = ubuf_ref[tm:tm + SUBLANES, :]

    k = _dot(xb, wr_ref[:, 0:d_attn])
    kt_ref[0] = k.T
    krm_ref[...] = k.astype(BF16)
    for blk in range(tm // MOBA_BLOCK):
        ksum = jnp.sum(k[blk * MOBA_BLOCK:(blk + 1) * MOBA_BLOCK, :], axis=0, keepdims=True)
        kmean_ref[0, pl.ds(tb * (tm // MOBA_BLOCK) + blk, 1), :] = ksum * (1.0 / MOBA_BLOCK)

    qt_ref[0] = (_dot_nt(wt_ref[0:d_attn, :], xb) * (LOG2E / HEAD_DIM ** 0.5)).astype(BF16)
    vt_ref[0] = _dot_nt(wt_ref[2 * d_attn:3 * d_attn, :], xb)


def _prompt_inproj(x2d, wt, wr, w_conv, *, bsz, seq, tm):
    n, d_model = x2d.shape
    d_attn = wt.shape[0] // 3
    d_conv = (wr.shape[1] - d_attn) // 3
    tpb = seq // tm
    nb = seq // MOBA_BLOCK
    kern = functools.partial(_inproj_kernel, tm=tm, tiles_per_batch=tpb, d_attn=d_attn, d_conv=d_conv)
    const = lambda t: (0, 0)
    ct_spec = lambda: pl.BlockSpec((1, d_attn, tm), lambda t: (t // tpb, 0, t % tpb))
    return pl.pallas_call(
        kern,
        grid=(n // tm,),
        in_specs=[
            pl.BlockSpec((tm, d_model), lambda t: (t, 0)),
            pl.BlockSpec(wt.shape, const),
            pl.BlockSpec(wr.shape, const),
            pl.BlockSpec(w_conv.shape, const),
        ],
        out_specs=[
            ct_spec(), ct_spec(), ct_spec(),
            pl.BlockSpec((tm, d_attn), lambda t: (t, 0)),
            pl.BlockSpec((1, nb, d_attn), lambda t: (t // tpb, 0, 0)),
            pl.BlockSpec((tm, d_conv), lambda t: (t, 0)),
            pl.BlockSpec((1, CONV_W - 1, d_conv), lambda t: (t // tpb, 0, 0)),
        ],
        out_shape=[
            jax.ShapeDtypeStruct((bsz, d_attn, seq), BF16),
            jax.ShapeDtypeStruct((bsz, d_attn, seq), F32),
            jax.ShapeDtypeStruct((bsz, d_attn, seq), F32),
            jax.ShapeDtypeStruct((n, d_attn), BF16),
            jax.ShapeDtypeStruct((bsz, nb, d_attn), F32),
            jax.ShapeDtypeStruct((n, d_conv), BF16),
            jax.ShapeDtypeStruct((bsz, CONV_W - 1, d_conv), F32),
        ],
        scratch_shapes=[pltpu.VMEM((tm + 2 * SUBLANES, d_conv), F32)],
        compiler_params=pltpu.CompilerParams(
            dimension_semantics=("arbitrary",), vmem_limit_bytes=56 * MIB),
        name="prompt_inproj",
    )(x2d, wt, wr, w_conv)


BIAS_CH0 = 16
GATE_PREFETCH = 2
DOWN_LAG = 7
DENOM_ROWS = 16


def _key_bias_channels(seq):
    pos = jnp.arange(seq, dtype=jnp.int32)[:, None]
    c = jnp.arange(2 * HEAD_DIM, dtype=jnp.int32)[None, :] - BIAS_CH0
    kblk, koff = pos // MOBA_BLOCK, pos % MOBA_BLOCK
    ext = jnp.where(c + BIAS_CH0 < SUBLANES, (kblk == c + BIAS_CH0).astype(F32), 0.0)
    ext = jnp.where((c >= 0) & (c < 3), koff.astype(F32), ext)
    ext = jnp.where((c >= 3) & (c < 6), (kblk * MOBA_BLOCK).astype(F32), ext)
    ext = jnp.where((c >= 6) & (c < 12), 1.0, ext)
    return ext.astype(BF16)


def _query_alibi_channels(coef, block, nq):
    row = lax.broadcasted_iota(jnp.int32, (2 * SUBLANES, nq), 0)
    qoff = lax.broadcasted_iota(jnp.int32, (2 * SUBLANES, nq), 1).astype(F32)
    cb = jnp.full((2 * SUBLANES, nq), 1.0, F32) * coef
    val = jnp.where(row < 6, cb,
                    jnp.where(row < 9, -cb * qoff,
                              jnp.where(row < 12, -cb * float(block * MOBA_BLOCK), 0.0)))
    hi = val.astype(BF16).astype(F32)
    mid = (val - hi).astype(BF16).astype(F32)
    lo = ((val - hi) - mid).astype(BF16).astype(F32)
    part = row % 3
    return jnp.where(part == 0, hi, jnp.where(part == 1, mid, lo))


def _selection_bias(g, n_past):
    sub = lax.broadcasted_iota(jnp.int32, g.shape, 0)
    rank = jnp.zeros(g.shape, jnp.int32)
    for jp in range(n_past):
        row = g[jp:jp + 1, :]
        beats = (row > g) | ((row == g) & (jp < sub))
        rank = rank + jnp.where(beats, 1, 0)
    sel = ((sub < n_past) & (rank < MOBA_TOPK)) | (sub == n_past)
    return jnp.where(sel, 0.0, NEG).astype(F32)


def _gate_wave_copies(pt_ref, kc_ref, buf_ref, sem_ref, wave, slot, *, pages_per_wave, pages_per_seq):
    first = wave * pages_per_wave
    seq_idx = first // pages_per_seq
    pg0 = first % pages_per_seq
    return [pltpu.make_async_copy(kc_ref.at[pt_ref[seq_idx, pg0 + pg]], buf_ref.at[slot, pg], sem_ref.at[slot])
            for pg in range(pages_per_wave)]


def _sample_gate_wave(copies, wave, w, part, n_parts, n_waves, qcol_ref, buf_ref, g_ref,
                      *, pages_per_wave, pages_per_seq, n_heads):
    slot = w
    n_slots = buf_ref.shape[0]
    pages_per_block = MOBA_BLOCK // LANES
    blocks_per_part = pages_per_wave // pages_per_block // n_parts
    n_blocks = pages_per_seq // pages_per_block
    row0 = (wave * pages_per_wave % pages_per_seq) // pages_per_block

    if part == 0:
        for cp in copies(jnp.minimum(wave + GATE_PREFETCH, n_waves - 1), (w + GATE_PREFETCH) % n_slots):
            cp.start()
        for cp in copies(wave, slot):
            cp.wait()
        g_ref[:, n_blocks:, :] = jnp.zeros((n_heads, g_ref.shape[1] - n_blocks, LANES), F32)

    qcol = qcol_ref[0]
    for h in range(n_heads):
        qb = jnp.broadcast_to(qcol[h * HEAD_DIM:(h + 1) * HEAD_DIM, :], (HEAD_DIM, LANES))
        for bl in range(part * blocks_per_part, (part + 1) * blocks_per_part):
            ksum = None
            for pg in range(pages_per_block):
                kt = buf_ref[slot, bl * pages_per_block + pg, h * HEAD_DIM:(h + 1) * HEAD_DIM, :]
                ksum = kt if ksum is None else ksum + kt
            g_ref[h, pl.ds(row0 + bl, 1), :] = jnp.sum(qb * ksum, axis=0, keepdims=True)


def _sample_gate_topk(qcol_ref, kcol_ref, sel_ref, g_ref, *, n_heads, past_len):
    qcol = qcol_ref[0]
    kcol = kcol_ref[0]
    own = past_len // MOBA_BLOCK
    nrow = g_ref.shape[1]
    lane = lax.broadcasted_iota(jnp.int32, (nrow, LANES), 1)
    sub = lax.broadcasted_iota(jnp.int32, (nrow, LANES), 0)
    gate = jnp.full((nrow, LANES), -jnp.inf, F32)
    for h in range(n_heads):
        tot = jnp.sum(g_ref[h], axis=1, keepdims=True)
        own_dot = jnp.sum(qcol[h * HEAD_DIM:(h + 1) * HEAD_DIM, :] * kcol[h * HEAD_DIM:(h + 1) * HEAD_DIM, :],
                          axis=0, keepdims=True)
        tot = jnp.where(sub[:, 0:1] == own, own_dot, tot) * (1.0 / MOBA_BLOCK)
        gate = jnp.where(lane == h, tot, gate)
    gate = jnp.where(sub < own, gate, -jnp.inf)
    rows = []
    for _ in range(MOBA_TOPK):
        m = jnp.max(gate, axis=0, keepdims=True)
        idx = jnp.min(jnp.where(gate == m, sub, nrow), axis=0, keepdims=True)
        rows.append(idx)
        gate = jnp.where(sub == idx, -jnp.inf, gate)
    rows.append(jnp.zeros((SUBLANES - MOBA_TOPK, LANES), jnp.int32))
    sel_ref[0] = jnp.concatenate(rows, axis=0)


def _attn_kernel(pt_ref, qt_ref, k_ref, kx_ref, vt_ref, kmean_ref, kc_ref, qcol_ref, kcol_ref,
                 o_ref, sel_ref,
                 kk_ref, vtb_ref, pbuf_ref, g_ref, sem_ref,
                 *, nb, n_grid, waves_per_step, pages_per_wave, pages_per_seq, n_heads, past_len):
    pair = pl.program_id(1)
    step = pl.program_id(0) * pl.num_programs(1) + pair
    blk = MOBA_BLOCK
    pw = 2 * HEAD_DIM

    n_waves = n_grid * waves_per_step
    copies = functools.partial(_gate_wave_copies, pt_ref, kc_ref, pbuf_ref, sem_ref,
                               pages_per_wave=pages_per_wave, pages_per_seq=pages_per_seq)
    gate_wave = functools.partial(_sample_gate_wave, copies, qcol_ref=qcol_ref, buf_ref=pbuf_ref, g_ref=g_ref,
                                  n_waves=n_waves, pages_per_wave=pages_per_wave, pages_per_seq=pages_per_seq,
                                  n_heads=n_heads)

    @pl.when(step == 0)
    def _():
        for ahead in range(GATE_PREFETCH):
            for cp in copies(ahead, ahead):
                cp.start()

    kk_ref[:, 0:pw] = k_ref[...]
    kk_ref[:, pw:2 * pw] = kx_ref[...]
    vrows = HEAD_DIM + DENOM_ROWS
    ones_row = jnp.where(lax.broadcasted_iota(jnp.int32, (DENOM_ROWS, vt_ref.shape[2]), 0) == 0,
                         1.0, 0.0).astype(BF16)
    for e in range(2):
        vtb_ref[e * vrows:e * vrows + HEAD_DIM, :] = vt_ref[0, e * HEAD_DIM:(e + 1) * HEAD_DIM, :].astype(BF16)
        vtb_ref[e * vrows + HEAD_DIM:(e + 1) * vrows, :] = ones_row

    slope_even = jnp.where(pair == 0, 2.0 ** -1,
                           jnp.where(pair == 1, 2.0 ** -3,
                                     jnp.where(pair == 2, 2.0 ** -5, 2.0 ** -7))).astype(F32)
    causal = (lax.broadcasted_iota(jnp.int32, (blk, blk), 1)
              >= lax.broadcasted_iota(jnp.int32, (blk, blk), 0))
    zeros8 = jnp.zeros((SUBLANES, blk), F32)
    zeros_h = jnp.zeros((HEAD_DIM, blk), BF16)
    zeros_tail = jnp.zeros((pw - 2 * BIAS_CH0, blk), BF16)
    kmean = kmean_ref[0]

    def scores(i, e):
        sl = slope_even * (0.5 ** e)
        qh = qt_ref[0, e * HEAD_DIM:(e + 1) * HEAD_DIM, i * blk:(i + 1) * blk]
        if i > 0:
            gate = jnp.dot(kmean[:, e * HEAD_DIM:(e + 1) * HEAD_DIM], qh.astype(F32),
                           preferred_element_type=F32, precision=lax.Precision.HIGHEST)
        else:
            gate = zeros8
        selb = _selection_bias(gate, i)
        q_ext = jnp.concatenate([qh, zeros_h] if e == 0 else [zeros_h, qh], axis=0)
        q_ext = jnp.concatenate(
            [q_ext,
             jnp.concatenate([selb, zeros8], axis=0).astype(BF16),
             _query_alibi_channels(sl * LOG2E, i, blk).astype(BF16),
             zeros_tail], axis=0)
        s = _dot(kk_ref[0:(i + 1) * blk, :], q_ext)
        s_own = jnp.where(causal, s[i * blk:, :], NEG)
        return s_own if i == 0 else jnp.concatenate([s[0:i * blk, :], s_own], axis=0)

    def softmax(s):
        m = jnp.max(s, axis=0, keepdims=True)
        return (jnp.exp2(s - m).astype(BF16),)

    def weighted_values(i, e, pr):
        acc = _dot(vtb_ref[e * vrows:(e + 1) * vrows, 0:(i + 1) * blk], pr)
        return acc[0:HEAD_DIM, :] * (1.0 / acc[HEAD_DIM:HEAD_DIM + 1, :])

    order = [(i, e) for i in range(nb) for e in range(2)]
    n_steps = len(order)
    s_buf, p_buf, outs = {}, {}, []
    parts_per_wave = n_steps // waves_per_step
    work = [i + 1 for i, _ in order]
    gate_parts_at = {n: [] for n in range(n_steps)}
    for k in range(n_steps):
        target = (k + 0.5) * sum(work) / n_steps
        gate_parts_at[next(n for n in range(n_steps) if sum(work[:n + 1]) >= target)].append(k)
    for n in range(-1, n_steps + 1):
        for k in gate_parts_at.get(n + 1, ()):
            w, part = divmod(k, parts_per_wave)
            gate_wave(step * waves_per_step + w, w, part, parts_per_wave)
        if 0 <= n + 1 < n_steps:
            s_buf[n + 1] = scores(*order[n + 1])
        if 0 <= n < n_steps:
            p_buf[n] = softmax(s_buf.pop(n))
        if n >= 1:
            i, e = order[n - 1]
            outs.append(weighted_values(i, e, *p_buf.pop(n - 1)))
            if e == 1:
                o_ref[i * blk:(i + 1) * blk, :] = jnp.concatenate(outs, axis=0).T.astype(BF16)
                outs = []

    @pl.when((step + 1) * waves_per_step * pages_per_wave % pages_per_seq == 0)
    def _():
        _sample_gate_topk(qcol_ref, kcol_ref, sel_ref, g_ref, n_heads=n_heads, past_len=past_len)

    @pl.when(step == n_grid - 1)
    def _():
        for ahead in range(GATE_PREFETCH):
            for cp in copies(n_waves - 1, ahead):
                cp.wait()


def _prompt_attention_and_sample_gate(qt, krm, vt, kmean, page_table, kc, qcol, kcol,
                                      *, bsz, seq, past_len, pages_per_wave):
    d_attn = qt.shape[1]
    nb = seq // MOBA_BLOCK
    assert nb <= SUBLANES, "selection-bias channels hold one block per row of a vector register"
    n_pairs = d_attn // (2 * HEAD_DIM)
    n_heads = d_attn // HEAD_DIM
    pw = 2 * HEAD_DIM
    n_grid = bsz * n_pairs
    n_seq, pages_per_seq = page_table.shape
    pages_per_step = n_seq * pages_per_seq // n_grid
    assert pages_per_step * n_grid == n_seq * pages_per_seq and pages_per_seq % pages_per_step == 0
    waves_per_step = pages_per_step // pages_per_wave
    assert waves_per_step * pages_per_wave == pages_per_step and waves_per_step > GATE_PREFETCH
    assert (2 * nb) % waves_per_step == 0
    assert pages_per_wave % ((MOBA_BLOCK // LANES) * (2 * nb // waves_per_step)) == 0
    gate_rows = ((pages_per_seq * LANES // MOBA_BLOCK + 1 + SUBLANES - 1) // SUBLANES) * SUBLANES
    seq_of = lambda b, p: (b * n_pairs + p) * pages_per_step // pages_per_seq
    col_spec = pl.BlockSpec((1, d_attn, 1), lambda b, p, pt: (seq_of(b, p), 0, 0))
    kern = functools.partial(_attn_kernel, nb=nb, n_grid=n_grid, waves_per_step=waves_per_step,
                             pages_per_wave=pages_per_wave, pages_per_seq=pages_per_seq, n_heads=n_heads,
                             past_len=past_len)
    grid_spec = pltpu.PrefetchScalarGridSpec(
        num_scalar_prefetch=1,
        grid=(bsz, n_pairs),
        in_specs=[
            pl.BlockSpec((1, pw, seq), lambda b, p, pt: (b, p, 0)),
            pl.BlockSpec((seq, pw), lambda b, p, pt: (b, p)),
            pl.BlockSpec((seq, pw), lambda b, p, pt: (0, 0)),
            pl.BlockSpec((1, pw, seq), lambda b, p, pt: (b, p, 0)),
            pl.BlockSpec((1, nb, pw), lambda b, p, pt: (b, 0, p)),
            pl.BlockSpec(memory_space=pl.ANY),
            col_spec, col_spec,
        ],
        out_specs=[
            pl.BlockSpec((seq, pw), lambda b, p, pt: (b, p)),
            pl.BlockSpec((1, SUBLANES, LANES), lambda b, p, pt: (seq_of(b, p), 0, 0)),
        ],
        scratch_shapes=[
            pltpu.VMEM((seq, 2 * pw), BF16),
            pltpu.VMEM((pw + 2 * DENOM_ROWS, seq), BF16),
            pltpu.VMEM((waves_per_step, pages_per_wave, d_attn, LANES), F32),
            pltpu.VMEM((n_heads, gate_rows, LANES), F32),
            pltpu.SemaphoreType.DMA((waves_per_step,)),
        ],
    )
    return pl.pallas_call(
        kern,
        grid_spec=grid_spec,
        out_shape=[jax.ShapeDtypeStruct((bsz * seq, d_attn), BF16),
                   jax.ShapeDtypeStruct((n_seq, SUBLANES, LANES), jnp.int32)],
        compiler_params=pltpu.CompilerParams(
            dimension_semantics=("arbitrary", "arbitrary"), vmem_limit_bytes=48 * MIB),
        name="prompt_moba_attention",
    )(page_table, qt, krm, _key_bias_channels(seq), vt, kmean, kc, qcol, kcol)


def _ffn_kernel(x_ref, att_ref, cm_ref, wo_ref, g1_ref, b1_ref, wup_ref, wfc_ref, wdn_ref, g2_ref, b2_ref,
                y_ref, fst_ref,
                x1_ref, x1b_ref, acc_ref, h_ref, ua_ref, ug_ref, carry_ref,
                *, tm, tiles_per_batch, d_attn, d_ff, cw, alpha):
    t = pl.program_id(0)
    mix = _dot(att_ref[...], wo_ref[0:d_attn, :]) + _dot(cm_ref[...], wo_ref[d_attn:, :])
    x1 = _layer_norm(alpha * x_ref[...] + mix, g1_ref[...], b1_ref[...])
    x1_ref[...] = x1
    x1b_ref[...] = x1.astype(BF16)

    @pl.when(t % tiles_per_batch == 0)
    def _():
        carry_ref[...] = jnp.zeros(carry_ref.shape, F32)

    def up_project(col0, ubuf):
        ubuf[0:SUBLANES, :] = carry_ref[:, col0:col0 + cw]
        ubuf[SUBLANES:SUBLANES + tm, :] = _dot(x1b_ref[...], wup_ref[:, col0:col0 + cw])

    def token_conv(col0, ubuf):
        w = wfc_ref[:, col0:col0 + cw]
        conv = (w[0:1, :] * ubuf[SUBLANES - 2:SUBLANES - 2 + tm, :]
                + w[1:2, :] * ubuf[SUBLANES - 1:SUBLANES - 1 + tm, :]
                + w[2:3, :] * ubuf[SUBLANES:SUBLANES + tm, :])
        carry_ref[:, col0:col0 + cw] = ubuf[tm:tm + SUBLANES, :]
        fst_ref[0, :, col0:col0 + cw] = ubuf[SUBLANES + tm - (CONV_W - 1):SUBLANES + tm, :]
        return conv

    n_chunks = d_ff // cw
    for c in range(-1, n_chunks + DOWN_LAG):
        if 0 <= c + 1 < n_chunks:
            up_project((c + 1) * cw, ua_ref.at[(c + 1) % 2])
        if 0 <= c < n_chunks:
            a = _silu(token_conv(c * cw, ua_ref.at[c % 2]))
        if 0 <= c + 1 < n_chunks:
            up_project(d_ff + (c + 1) * cw, ug_ref.at[(c + 1) % 2])
        if 0 <= c < n_chunks:
            h_ref[:, c * cw:(c + 1) * cw] = (a * token_conv(d_ff + c * cw, ug_ref.at[c % 2])).astype(BF16)
        if c >= DOWN_LAG:
            k = c - DOWN_LAG
            part = _dot(h_ref[:, k * cw:(k + 1) * cw], wdn_ref[k * cw:(k + 1) * cw, :])
            if k == 0:
                acc_ref[...] = part
            else:
                acc_ref[...] += part

    y_ref[...] = _layer_norm(alpha * x1_ref[...] + acc_ref[...], g2_ref[...], b2_ref[...])


def _prompt_ffn(x2d, att, cm, wo, g1, b1, wup, wfc, wdn, g2, b2, *, bsz, seq, tm, cw, alpha):
    n, d_model = x2d.shape
    d_attn = att.shape[1]
    d_ff = wdn.shape[0]
    tpb = seq // tm
    kern = functools.partial(_ffn_kernel, tm=tm, tiles_per_batch=tpb, d_attn=d_attn, d_ff=d_ff, cw=cw,
                             alpha=alpha)
    const = lambda t: (0, 0)
    resident = lambda a: pl.BlockSpec(a.shape, const, pipeline_mode=pl.Buffered(1))
    row = lambda w: pl.BlockSpec((tm, w), lambda t: (t, 0))
    return pl.pallas_call(
        kern,
        grid=(n // tm,),
        in_specs=[row(d_model), row(d_attn), row(cm.shape[1]),
                  resident(wo), resident(g1), resident(b1), resident(wup), resident(wfc), resident(wdn),
                  resident(g2), resident(b2)],
        out_specs=[row(d_model),
                   pl.BlockSpec((1, CONV_W - 1, 2 * d_ff), lambda t: (t // tpb, 0, 0))],
        out_shape=[jax.ShapeDtypeStruct((n, d_model), F32),
                   jax.ShapeDtypeStruct((bsz, CONV_W - 1, 2 * d_ff), F32)],
        scratch_shapes=[
            pltpu.VMEM((tm, d_model), F32),
            pltpu.VMEM((tm, d_model), BF16),
            pltpu.VMEM((tm, d_model), F32),
            pltpu.VMEM((tm, d_ff), BF16),
            pltpu.VMEM((2, tm + 2 * SUBLANES, cw), F32),
            pltpu.VMEM((2, tm + 2 * SUBLANES, cw), F32),
            pltpu.VMEM((SUBLANES, 2 * d_ff), F32),
        ],
        compiler_params=pltpu.CompilerParams(
            dimension_semantics=("arbitrary",), vmem_limit_bytes=56 * MIB),
        name="prompt_outproj_ffn",
    )(x2d, att, cm, wo, g1, b1, wup, wfc, wdn, g2, b2)


def _sample_inproj_kernel(x_ref, w_ref, wc_ref, st0_ref, st1_ref,
                          q_ref, k_ref, v_ref, cm_ref, u_ref, *, d_attn, d_conv):
    z = jnp.dot(x_ref[...], w_ref[...], preferred_element_type=F32,
                precision=lax.Precision.HIGHEST)
    q_ref[...] = z[:, 0:d_attn] * 0.125
    k_ref[...] = z[:, d_attn:2 * d_attn]
    v_ref[...] = z[:, 2 * d_attn:3 * d_attn]
    o = 3 * d_attn
    gb = z[:, o:o + d_conv]
    u = z[:, o + d_conv:o + 2 * d_conv] * z[:, o + 2 * d_conv:o + 3 * d_conv]
    wc = wc_ref[...]
    conv = wc[0:1, :] * st0_ref[...] + wc[1:2, :] * st1_ref[...] + wc[2:3, :] * u
    cm_ref[...] = gb * conv
    u_ref[...] = u


def _sample_inproj(xs, w_in_f32, w_conv, st0, st1, *, d_attn, d_conv):
    b = xs.shape[0]
    kern = functools.partial(_sample_inproj_kernel, d_attn=d_attn, d_conv=d_conv)
    sd = lambda w: jax.ShapeDtypeStruct((b, w), F32)
    return pl.pallas_call(
        kern,
        out_shape=[sd(d_attn), sd(d_attn), sd(d_attn), sd(d_conv), sd(d_conv)],
        compiler_params=pltpu.CompilerParams(vmem_limit_bytes=48 * MIB),
        name="sample_inproj",
    )(xs, w_in_f32, w_conv, st0, st1)


def _sel_copies(sel_ref, pt_ref, kc_ref, vc_ref, kbuf_ref, vbuf_ref, sem_ref, g, slot, *, n_heads, group):
    pages_per_block = MOBA_BLOCK // LANES
    copies = []
    for j in range(group):
        b = g * group + j
        for h in range(n_heads):
            for r in range(MOBA_TOPK):
                blk = sel_ref[(b * n_heads + h) * MOBA_TOPK + r]
                for pg in range(pages_per_block):
                    phys = pt_ref[b, blk * pages_per_block + pg]
                    rows = pl.ds(h * HEAD_DIM, HEAD_DIM)
                    t = r * pages_per_block + pg
                    copies.append(pltpu.make_async_copy(kc_ref.at[phys, rows], kbuf_ref.at[slot, j, h, t],
                                                        sem_ref.at[0, slot]))
                    copies.append(pltpu.make_async_copy(vc_ref.at[phys, rows], vbuf_ref.at[slot, j, h, t],
                                                        sem_ref.at[1, slot]))
    return copies


def _sample_attn_kernel(sel_ref, pt_ref, kc_ref, vc_ref, qcol_ref, kcol_ref, vcol_ref, o_ref,
                        kbuf_ref, vbuf_ref, sem_ref, *, n_steps, n_heads, group, past_len):
    g = pl.program_id(0)
    slot = g % 2
    pages_per_block = MOBA_BLOCK // LANES
    n_tiles = MOBA_TOPK * pages_per_block
    copies = functools.partial(_sel_copies, sel_ref, pt_ref, kc_ref, vc_ref, kbuf_ref, vbuf_ref, sem_ref,
                               n_heads=n_heads, group=group)

    @pl.when(g == 0)
    def _():
        for cp in copies(g, slot):
            cp.start()

    @pl.when(g + 1 < n_steps)
    def _():
        for cp in copies(g + 1, 1 - slot):
            cp.start()

    for cp in copies(g, slot):
        cp.wait()

    lane = lax.broadcasted_iota(jnp.int32, (1, LANES), 1)
    chains = [(j, h) for j in range(group) for h in range(n_heads)]
    hs = [slice(h * HEAD_DIM, (h + 1) * HEAD_DIM) for h in range(n_heads)]
    qcol = [qcol_ref[j] for j in range(group)]
    kcol = [kcol_ref[j] for j in range(group)]
    vcol = [vcol_ref[j] for j in range(group)]

    s_own, s_all = {}, {}
    for j, h in chains:
        b = g * group + j
        qb = jnp.broadcast_to(qcol[j][hs[h], :], (HEAD_DIM, LANES))
        s_own[j, h] = jnp.sum(qcol[j][hs[h], :] * kcol[j][hs[h], :], axis=0, keepdims=True)
        rows = []
        for t in range(n_tiles):
            blk = sel_ref[(b * n_heads + h) * MOBA_TOPK + t // pages_per_block]
            pos0 = blk * MOBA_BLOCK + (t % pages_per_block) * LANES
            dist = (past_len - pos0 - lane).astype(F32)
            rows.append(jnp.sum(qb * kbuf_ref[slot, j, h, t], axis=0, keepdims=True) - 2.0 ** -(h + 1) * dist)
        s_all[j, h] = jnp.concatenate(rows, axis=0)
    m = {c: jnp.maximum(s_own[c], jnp.max(s_all[c], axis=(0, 1), keepdims=True)) for c in chains}
    pr = {c: jnp.exp(s_all[c] - m[c]) for c in chains}
    p_own = {c: jnp.exp(s_own[c] - m[c]) for c in chains}
    l = {c: p_own[c] + jnp.sum(pr[c], axis=(0, 1), keepdims=True) for c in chains}
    for j, h in chains:
        acc = p_own[j, h] * vcol[j][hs[h], :]
        wv = None
        for t in range(n_tiles):
            part = vbuf_ref[slot, j, h, t] * pr[j, h][t:t + 1, :]
            wv = part if wv is None else wv + part
        acc = acc + jnp.sum(wv, axis=1, keepdims=True)
        o_ref[j, hs[h], :] = acc * (1.0 / l[j, h])


def _sample_attention(sel_flat, page_table, kc, vc, qcol, kcol, vcol, *, n_heads, past_len):
    bsz = page_table.shape[0]
    d_attn = n_heads * HEAD_DIM
    n_tiles = MOBA_TOPK * (MOBA_BLOCK // LANES)
    group = next(g for g in (4, 2, 1) if bsz % g == 0)
    n_steps = bsz // group
    kern = functools.partial(_sample_attn_kernel, n_steps=n_steps, n_heads=n_heads, group=group,
                             past_len=past_len)
    col_spec = pl.BlockSpec((group, d_attn, 1), lambda g, sel, pt: (g, 0, 0))
    any_spec = pl.BlockSpec(memory_space=pl.ANY)
    grid_spec = pltpu.PrefetchScalarGridSpec(
        num_scalar_prefetch=2,
        grid=(n_steps,),
        in_specs=[any_spec, any_spec, col_spec, col_spec, col_spec],
        out_specs=col_spec,
        scratch_shapes=[
            pltpu.VMEM((2, group, n_heads, n_tiles, HEAD_DIM, LANES), F32),
            pltpu.VMEM((2, group, n_heads, n_tiles, HEAD_DIM, LANES), F32),
            pltpu.SemaphoreType.DMA((2, 2)),
        ],
    )
    return pl.pallas_call(
        kern,
        grid_spec=grid_spec,
        out_shape=jax.ShapeDtypeStruct((bsz, d_attn, 1), F32),
        compiler_params=pltpu.CompilerParams(
            dimension_semantics=("arbitrary",), vmem_limit_bytes=48 * MIB),
        name="sample_moba_attention",
    )(sel_flat, page_table, kc, vc, qcol, kcol, vcol)


def _sample_ffn_kernel(x_ref, att_ref, cm_ref, wo_ref, g1_ref, b1_ref, wup_ref, wfc_ref, sf0_ref, sf1_ref,
                       wdn_ref, g2_ref, b2_ref, y_ref, up_ref, *, d_attn, d_ff, alpha):
    mix = (_dot(att_ref[...].astype(BF16), wo_ref[0:d_attn, :])
           + _dot(cm_ref[...].astype(BF16), wo_ref[d_attn:, :]))
    x1 = _layer_norm(alpha * x_ref[...] + mix, g1_ref[...], b1_ref[...])
    up = _dot(x1.astype(BF16), wup_ref[...])
    up_ref[...] = up
    w = wfc_ref[...]
    conv = w[0:1, :] * sf0_ref[...] + w[1:2, :] * sf1_ref[...] + w[2:3, :] * up
    hcol = (_silu(conv[:, 0:d_ff]) * conv[:, d_ff:]).astype(BF16)
    f = _dot(hcol, wdn_ref[...])
    y_ref[...] = _layer_norm(alpha * x1 + f, g2_ref[...], b2_ref[...])


def _sample_ffn(xs, att, cm, wo, g1, b1, wup, wfc, sf0, sf1, wdn, g2, b2, *, alpha):
    b, d_model = xs.shape
    d_ff = wdn.shape[0]
    kern = functools.partial(_sample_ffn_kernel, d_attn=att.shape[1], d_ff=d_ff, alpha=alpha)
    return pl.pallas_call(
        kern,
        out_shape=[jax.ShapeDtypeStruct((b, d_model), F32), jax.ShapeDtypeStruct((b, 2 * d_ff), F32)],
        compiler_params=pltpu.CompilerParams(vmem_limit_bytes=48 * MIB),
        name="sample_outproj_ffn",
    )(xs, att, cm, wo, g1, b1, wup, wfc, sf0, sf1, wdn, g2, b2)


def kernel(x_prompt, x_sample, cache_k, cache_v, state_conv, state_ffn_conv, page_table, w_in, w_conv, w_out,
           ln1_g, ln1_b, w_ffn_up, w_ffn_conv, w_ffn_down, ln2_g, ln2_b):
    depth = w_in.shape[0]
    assert depth == 1, "single-layer step"
    bsz, seq, d_model = x_prompt.shape
    dec_b, dec_t, _ = x_sample.shape
    assert dec_t == 1
    _, n_phys, page_size, n_heads, head_dim = cache_k.shape
    assert head_dim == HEAD_DIM and page_size == LANES
    d_attn = n_heads * head_dim
    d_conv = (w_in.shape[2] - 3 * d_attn) // 3
    d_ff = w_ffn_down.shape[1]
    past_len = page_table.shape[1] * page_size
    alpha = (2.0 * depth) ** 0.25
    row = lambda a: a.reshape(1, -1)

    w_in_b = w_in[0].astype(BF16)
    wt = w_in_b[:, :3 * d_attn].T
    wr = w_in_b[:, d_attn:2 * d_attn]
    wr = jnp.concatenate([wr, w_in_b[:, 3 * d_attn:]], axis=1)
    wo = w_out[0].astype(BF16)
    wup = w_ffn_up[0].astype(BF16)
    wdn = w_ffn_down[0].astype(BF16)
    wc, wfc = w_conv[0], w_ffn_conv[0]
    g1, b1, g2, b2 = row(ln1_g[0]), row(ln1_b[0]), row(ln2_g[0]), row(ln2_b[0])

    xs = x_sample.reshape(dec_b, d_model)
    st0, st1 = state_conv[0, :, 0, :], state_conv[0, :, 1, :]
    q_s, k_s, v_s, cm_s, u_s = _sample_inproj(xs, w_in[0], wc, st0, st1, d_attn=d_attn, d_conv=d_conv)
    kc = jnp.transpose(cache_k[0], (0, 2, 3, 1)).reshape(n_phys, d_attn, page_size)
    vc = jnp.transpose(cache_v[0], (0, 2, 3, 1)).reshape(n_phys, d_attn, page_size)
    qcol, kcol, vcol = q_s[:, :, None], k_s[:, :, None], v_s[:, :, None]

    x2d = x_prompt.reshape(bsz * seq, d_model)
    qt, kt, vt, krm, kmean, cm, conv_p = _prompt_inproj(x2d, wt, wr, wc, bsz=bsz, seq=seq, tm=1024)
    att, sel = _prompt_attention_and_sample_gate(qt, krm, vt, kmean, page_table, kc, qcol, kcol,
                                                 bsz=bsz, seq=seq, past_len=past_len, pages_per_wave=16)
    y_p, ffn_p = _prompt_ffn(x2d, att, cm, wo, g1, b1, wup, wfc, wdn, g2, b2,
                             bsz=bsz, seq=seq, tm=512, cw=256, alpha=alpha)
    to_cache = lambda a: jnp.transpose(a.reshape(1, bsz, n_heads, head_dim, seq), (0, 1, 4, 2, 3))
    k_prompt, v_prompt = to_cache(kt), to_cache(vt)

    sel_flat = jnp.transpose(sel[:, :MOBA_TOPK, :n_heads], (0, 2, 1)).reshape(-1)
    att_s = _sample_attention(sel_flat, page_table, kc, vc, qcol, kcol, vcol,
                              n_heads=n_heads, past_len=past_len)
    sf0, sf1 = state_ffn_conv[0, :, 0, :], state_ffn_conv[0, :, 1, :]
    y_s, up_s = _sample_ffn(xs, att_s.reshape(dec_b, d_attn), cm_s, wo, g1, b1, wup, wfc, sf0, sf1,
                            wdn, g2, b2, alpha=alpha)

    return (y_p.reshape(bsz, seq, d_model),
            y_s.reshape(dec_b, 1, d_model),
            k_prompt, v_prompt,
            conv_p[None], ffn_p[None],
            k_s.reshape(1, dec_b, 1, n_heads, head_dim), v_s.reshape(1, dec_b, 1, n_heads, head_dim),
            jnp.stack([st1, u_s], axis=1)[None],
            jnp.stack([sf1, up_s], axis=1)[None])
```

```python
import functools

import jax
import jax.numpy as jnp
from jax import lax
from jax.experimental import pallas as pl
from jax.experimental.pallas import tpu as pltpu

F32 = jnp.float32
BF16 = jnp.bfloat16

HEAD_DIM = 64
MOBA_BLOCK = 256
MOBA_TOPK = 3
CONV_W = 3
LN_EPS = 1e-5
LOG2E = 1.4426950408889634
NEG = -1e30
SUBLANES = 8
LANES = 128
MIB = 1024 * 1024

_NT = (((1,), (1,)), ((), ()))


def _dot(a, b):
    return jnp.dot(a, b, preferred_element_type=F32)


def _dot_nt(a, b):
    return lax.dot_general(a, b, _NT, preferred_element_type=F32)


def _layer_norm(x, g, b):
    mu = jnp.mean(x, axis=-1, keepdims=True)
    xc = x - mu
    var = jnp.mean(xc * xc, axis=-1, keepdims=True)
    return xc * lax.rsqrt(var + LN_EPS) * g + b


def _silu(a):
    return a * jax.nn.sigmoid(a)


def _inproj_kernel(x_ref, wt_ref, wr_ref, wc_ref,
                   qt_ref, kt_ref, vt_ref, krm_ref, kmean_ref, cm_ref, cst_ref,
                   ubuf_ref, *, tm, tiles_per_batch, d_attn, d_conv):
    t = pl.program_id(0)
    tb = t % tiles_per_batch
    @pl.when(tb == 0)
    def _():
        ubuf_ref[0:SUBLANES, :] = jnp.zeros((SUBLANES, d_conv), F32)

    xb = x_ref[...].astype(BF16)

    gc = _dot(xb, wr_ref[:, d_attn + d_conv:d_attn + 2 * d_conv])
    hh = _dot(xb, wr_ref[:, d_attn + 2 * d_conv:d_attn + 3 * d_conv])
    u = gc * hh
    gb = _dot(xb, wr_ref[:, d_attn:d_attn + d_conv])
    ubuf_ref[SUBLANES:SUBLANES + tm, :] = u
    wc = wc_ref[...]
    conv = (wc[0:1, :] * ubuf_ref[SUBLANES - 2:SUBLANES - 2 + tm, :]
            + wc[1:2, :] * ubuf_ref[SUBLANES - 1:SUBLANES - 1 + tm, :]
            + wc[2:3, :] * u)
    cm_ref[...] = (gb * conv).astype(BF16)
    cst_ref[0] = ubuf_ref[SUBLANES + tm - (CONV_W - 1):SUBLANES + tm, :]
    ubuf_ref[0:SUBLANES, :] = ubuf_ref[tm:tm + SUBLANES, :]

    k = _dot(xb, wr_ref[:, 0:d_attn])
    kt_ref[0] = k.T
    krm_ref[...] = k.astype(BF16)
    for blk in range(tm // MOBA_BLOCK):
        ksum = jnp.sum(k[blk * MOBA_BLOCK:(blk + 1) * MOBA_BLOCK, :], axis=0, keepdims=True)
        kmean_ref[0, pl.ds(tb * (tm // MOBA_BLOCK) + blk, 1), :] = ksum * (1.0 / MOBA_BLOCK)

    qt_ref[0] = (_dot_nt(wt_ref[0:d_attn, :], xb) * (LOG2E / HEAD_DIM ** 0.5)).astype(BF16)
    vt_ref[0] = _dot_nt(wt_ref[2 * d_attn:3 * d_attn, :], xb)


def _prompt_inproj(x2d, wt, wr, w_conv, *, bsz, seq, tm):
    n, d_model = x2d.shape
    d_attn = wt.shape[0] // 3
    d_conv = (wr.shape[1] - d_attn) // 3
    tpb = seq // tm
    nb = seq // MOBA_BLOCK
    kern = functools.partial(_inproj_kernel, tm=tm, tiles_per_batch=tpb, d_attn=d_attn, d_conv=d_conv)
    const = lambda t: (0, 0)
    ct_spec = lambda: pl.BlockSpec((1, d_attn, tm), lambda t: (t // tpb, 0, t % tpb))
    return pl.pallas_call(
        kern,
        grid=(n // tm,),
        in_specs=[
            pl.BlockSpec((tm, d_model), lambda t: (t, 0)),
            pl.BlockSpec(wt.shape, const),
            pl.BlockSpec(wr.shape, const),
            pl.BlockSpec(w_conv.shape, const),
        ],
        out_specs=[
            ct_spec(), ct_spec(), ct_spec(),
            pl.BlockSpec((tm, d_attn), lambda t: (t, 0)),
            pl.BlockSpec((1, nb, d_attn), lambda t: (t // tpb, 0, 0)),
            pl.BlockSpec((tm, d_conv), lambda t: (t, 0)),
            pl.BlockSpec((1, CONV_W - 1, d_conv), lambda t: (t // tpb, 0, 0)),
        ],
        out_shape=[
            jax.ShapeDtypeStruct((bsz, d_attn, seq), BF16),
            jax.ShapeDtypeStruct((bsz, d_attn, seq), F32),
            jax.ShapeDtypeStruct((bsz, d_attn, seq), F32),
            jax.ShapeDtypeStruct((n, d_attn), BF16),
            jax.ShapeDtypeStruct((bsz, nb, d_attn), F32),
            jax.ShapeDtypeStruct((n, d_conv), BF16),
            jax.ShapeDtypeStruct((bsz, CONV_W - 1, d_conv), F32),
        ],
        scratch_shapes=[pltpu.VMEM((tm + 2 * SUBLANES, d_conv), F32)],
        compiler_params=pltpu.CompilerParams(
            dimension_semantics=("arbitrary",), vmem_limit_bytes=56 * MIB),
        name="prompt_inproj",
    )(x2d, wt, wr, w_conv)


BIAS_CH0 = 16
GATE_PREFETCH = 1
DOWN_LAG = 7
DENOM_ROWS = 16


def _key_bias_channels(seq):
    pos = jnp.arange(seq, dtype=jnp.int32)[:, None]
    c = jnp.arange(2 * HEAD_DIM, dtype=jnp.int32)[None, :] - BIAS_CH0
    kblk, koff = pos // MOBA_BLOCK, pos % MOBA_BLOCK
    ext = jnp.where(c + BIAS_CH0 < SUBLANES, (kblk == c + BIAS_CH0).astype(F32), 0.0)
    ext = jnp.where((c >= 0) & (c < 3), koff.astype(F32), ext)
    ext = jnp.where((c >= 3) & (c < 6), (kblk * MOBA_BLOCK).astype(F32), ext)
    ext = jnp.where((c >= 6) & (c < 12), 1.0, ext)
    return ext.astype(BF16)


def _query_alibi_channels(coef, block, nq):
    row = lax.broadcasted_iota(jnp.int32, (2 * SUBLANES, nq), 0)
    qoff = lax.broadcasted_iota(jnp.int32, (2 * SUBLANES, nq), 1).astype(F32)
    cb = jnp.full((2 * SUBLANES, nq), 1.0, F32) * coef
    val = jnp.where(row < 6, cb,
                    jnp.where(row < 9, -cb * qoff,
                              jnp.where(row < 12, -cb * float(block * MOBA_BLOCK), 0.0)))
    hi = val.astype(BF16).astype(F32)
    mid = (val - hi).astype(BF16).astype(F32)
    lo = ((val - hi) - mid).astype(BF16).astype(F32)
    part = row % 3
    return jnp.where(part == 0, hi, jnp.where(part == 1, mid, lo))


def _selection_bias(g, n_past):
    sub = lax.broadcasted_iota(jnp.int32, g.shape, 0)
    rank = jnp.zeros(g.shape, jnp.int32)
    for jp in range(n_past):
        row = g[jp:jp + 1, :]
        beats = (row > g) | ((row == g) & (jp < sub))
        rank = rank + jnp.where(beats, 1, 0)
    sel = ((sub < n_past) & (rank < MOBA_TOPK)) | (sub == n_past)
    return jnp.where(sel, 0.0, NEG).astype(F32)


def _gate_wave_copies(pt_ref, kc_ref, buf_ref, sem_ref, wave, slot, *, pages_per_wave, pages_per_seq):
    first = wave * pages_per_wave
    seq_idx = first // pages_per_seq
    pg0 = first % pages_per_seq
    return [pltpu.make_async_copy(kc_ref.at[pt_ref[seq_idx, pg0 + pg]], buf_ref.at[slot, pg], sem_ref.at[slot])
            for pg in range(pages_per_wave)]


def _sample_gate_wave(copies, wave, w, part, n_parts, n_waves, qcol_ref, buf_ref, g_ref,
                      *, pages_per_wave, pages_per_seq, n_heads):
    slot = w
    n_slots = buf_ref.shape[0]
    pages_per_block = MOBA_BLOCK // LANES
    blocks_per_part = pages_per_wave // pages_per_block // n_parts
    n_blocks = pages_per_seq // pages_per_block
    row0 = (wave * pages_per_wave % pages_per_seq) // pages_per_block

    if part == 0:
        for cp in copies(jnp.minimum(wave + GATE_PREFETCH, n_waves - 1), (w + GATE_PREFETCH) % n_slots):
            cp.start()
        for cp in copies(wave, slot):
            cp.wait()
        g_ref[:, n_blocks:, :] = jnp.zeros((n_heads, g_ref.shape[1] - n_blocks, LANES), F32)

    qcol = qcol_ref[0]
    for h in range(n_heads):
        qb = jnp.broadcast_to(qcol[h * HEAD_DIM:(h + 1) * HEAD_DIM, :], (HEAD_DIM, LANES))
        for bl in range(part * blocks_per_part, (part + 1) * blocks_per_part):
            ksum = None
            for pg in range(pages_per_block):
                kt = buf_ref[slot, bl * pages_per_block + pg, h * HEAD_DIM:(h + 1) * HEAD_DIM, :]
                ksum = kt if ksum is None else ksum + kt
            g_ref[h, pl.ds(row0 + bl, 1), :] = jnp.sum(qb * ksum, axis=0, keepdims=True)


def _sample_gate_topk(qcol_ref, kcol_ref, sel_ref, g_ref, *, n_heads, past_len):
    qcol = qcol_ref[0]
    kcol = kcol_ref[0]
    own = past_len // MOBA_BLOCK
    nrow = g_ref.shape[1]
    lane = lax.broadcasted_iota(jnp.int32, (nrow, LANES), 1)
    sub = lax.broadcasted_iota(jnp.int32, (nrow, LANES), 0)
    gate = jnp.full((nrow, LANES), -jnp.inf, F32)
    for h in range(n_heads):
        tot = jnp.sum(g_ref[h], axis=1, keepdims=True)
        own_dot = jnp.sum(qcol[h * HEAD_DIM:(h + 1) * HEAD_DIM, :] * kcol[h * HEAD_DIM:(h + 1) * HEAD_DIM, :],
                          axis=0, keepdims=True)
        tot = jnp.where(sub[:, 0:1] == own, own_dot, tot) * (1.0 / MOBA_BLOCK)
        gate = jnp.where(lane == h, tot, gate)
    gate = jnp.where(sub < own, gate, -jnp.inf)
    rows = []
    for _ in range(MOBA_TOPK):
        m = jnp.max(gate, axis=0, keepdims=True)
        idx = jnp.min(jnp.where(gate == m, sub, nrow), axis=0, keepdims=True)
        rows.append(idx)
        gate = jnp.where(sub == idx, -jnp.inf, gate)
    rows.append(jnp.zeros((SUBLANES - MOBA_TOPK, LANES), jnp.int32))
    sel_ref[0] = jnp.concatenate(rows, axis=0)


def _attn_kernel(pt_ref, qt_ref, k_ref, kx_ref, vt_ref, kmean_ref, kc_ref, qcol_ref, kcol_ref,
                 o_ref, sel_ref,
                 kk_ref, vtb_ref, pbuf_ref, g_ref, sem_ref,
                 *, nb, n_grid, waves_per_step, pages_per_wave, pages_per_seq, n_heads, past_len):
    pair = pl.program_id(1)
    step = pl.program_id(0) * pl.num_programs(1) + pair
    blk = MOBA_BLOCK
    pw = 2 * HEAD_DIM

    n_waves = n_grid * waves_per_step
    copies = functools.partial(_gate_wave_copies, pt_ref, kc_ref, pbuf_ref, sem_ref,
                               pages_per_wave=pages_per_wave, pages_per_seq=pages_per_seq)
    gate_wave = functools.partial(_sample_gate_wave, copies, qcol_ref=qcol_ref, buf_ref=pbuf_ref, g_ref=g_ref,
                                  n_waves=n_waves, pages_per_wave=pages_per_wave, pages_per_seq=pages_per_seq,
                                  n_heads=n_heads)

    @pl.when(step == 0)
    def _():
        for ahead in range(GATE_PREFETCH):
            for cp in copies(ahead, ahead):
                cp.start()

    kk_ref[:, 0:pw] = k_ref[...]
    kk_ref[:, pw:2 * pw] = kx_ref[...]
    vrows = HEAD_DIM + DENOM_ROWS
    ones_row = jnp.where(lax.broadcasted_iota(jnp.int32, (DENOM_ROWS, vt_ref.shape[2]), 0) == 0,
                         1.0, 0.0).astype(BF16)
    for e in range(2):
        vtb_ref[e * vrows:e * vrows + HEAD_DIM, :] = vt_ref[0, e * HEAD_DIM:(e + 1) * HEAD_DIM, :].astype(BF16)
        vtb_ref[e * vrows + HEAD_DIM:(e + 1) * vrows, :] = ones_row

    slope_even = jnp.where(pair == 0, 2.0 ** -1,
                           jnp.where(pair == 1, 2.0 ** -3,
                                     jnp.where(pair == 2, 2.0 ** -5, 2.0 ** -7))).astype(F32)
    causal = (lax.broadcasted_iota(jnp.int32, (blk, blk), 1)
              >= lax.broadcasted_iota(jnp.int32, (blk, blk), 0))
    zeros8 = jnp.zeros((SUBLANES, blk), F32)
    zeros_h = jnp.zeros((HEAD_DIM, blk), BF16)
    zeros_tail = jnp.zeros((pw - 2 * BIAS_CH0, blk), BF16)
    kmean = kmean_ref[0]

    def scores(i, e):
        sl = slope_even * (0.5 ** e)
        qh = qt_ref[0, e * HEAD_DIM:(e + 1) * HEAD_DIM, i * blk:(i + 1) * blk]
        if i > 0:
            gate = jnp.dot(kmean[:, e * HEAD_DIM:(e + 1) * HEAD_DIM], qh.astype(F32),
                           preferred_element_type=F32, precision=lax.Precision.HIGHEST)
        else:
            gate = zeros8
        selb = _selection_bias(gate, i)
        q_ext = jnp.concatenate([qh, zeros_h] if e == 0 else [zeros_h, qh], axis=0)
        q_ext = jnp.concatenate(
            [q_ext,
             jnp.concatenate([selb, zeros8], axis=0).astype(BF16),
             _query_alibi_channels(sl * LOG2E, i, blk).astype(BF16),
             zeros_tail], axis=0)
        s = _dot(kk_ref[0:(i + 1) * blk, :], q_ext)
        s_own = jnp.where(causal, s[i * blk:, :], NEG)
        return s_own if i == 0 else jnp.concatenate([s[0:i * blk, :], s_own], axis=0)

    def softmax(s):
        m = jnp.max(s, axis=0, keepdims=True)
        return (jnp.exp2(s - m).astype(BF16),)

    def weighted_values(i, e, pr):
        acc = _dot(vtb_ref[e * vrows:(e + 1) * vrows, 0:(i + 1) * blk], pr)
        return acc[0:HEAD_DIM, :] * (1.0 / acc[HEAD_DIM:HEAD_DIM + 1, :])

    order = [(i, e) for i in range(nb) for e in range(2)]
    n_steps = len(order)
    s_buf, p_buf, outs = {}, {}, []
    parts_per_wave = n_steps // waves_per_step
    work = [i + 1 for i, _ in order]
    gate_parts_at = {n: [] for n in range(n_steps)}
    for k in range(n_steps):
        target = (k + 0.5) * sum(work) / n_steps
        gate_parts_at[next(n for n in range(n_steps) if sum(work[:n + 1]) >= target)].append(k)
    for n in range(-1, n_steps + 1):
        for k in gate_parts_at.get(n + 1, ()):
            w, part = divmod(k, parts_per_wave)
            gate_wave(step * waves_per_step + w, w, part, parts_per_wave)
        if 0 <= n + 1 < n_steps:
            s_buf[n + 1] = scores(*order[n + 1])
        if 0 <= n < n_steps:
            p_buf[n] = softmax(s_buf.pop(n))
        if n >= 1:
            i, e = order[n - 1]
            outs.append(weighted_values(i, e, *p_buf.pop(n - 1)))
            if e == 1:
                o_ref[i * blk:(i + 1) * blk, :] = jnp.concatenate(outs, axis=0).T.astype(BF16)
                outs = []

    @pl.when((step + 1) * waves_per_step * pages_per_wave % pages_per_seq == 0)
    def _():
        _sample_gate_topk(qcol_ref, kcol_ref, sel_ref, g_ref, n_heads=n_heads, past_len=past_len)

    @pl.when(step == n_grid - 1)
    def _():
        for ahead in range(GATE_PREFETCH):
            for cp in copies(n_waves - 1, ahead):
                cp.wait()


def _prompt_attention_and_sample_gate(qt, krm, vt, kmean, page_table, kc, qcol, kcol,
                                      *, bsz, seq, past_len, pages_per_wave):
    d_attn = qt.shape[1]
    nb = seq // MOBA_BLOCK
    assert nb <= SUBLANES, "selection-bias channels hold one block per row of a vector register"
    n_pairs = d_attn // (2 * HEAD_DIM)
    n_heads = d_attn // HEAD_DIM
    pw = 2 * HEAD_DIM
    n_grid = bsz * n_pairs
    n_seq, pages_per_seq = page_table.shape
    pages_per_step = n_seq * pages_per_seq // n_grid
    assert pages_per_step * n_grid == n_seq * pages_per_seq and pages_per_seq % pages_per_step == 0
    waves_per_step = pages_per_step // pages_per_wave
    assert waves_per_step * pages_per_wave == pages_per_step and waves_per_step > GATE_PREFETCH
    assert (2 * nb) % waves_per_step == 0
    assert pages_per_wave % ((MOBA_BLOCK // LANES) * (2 * nb // waves_per_step)) == 0
    gate_rows = ((pages_per_seq * LANES // MOBA_BLOCK + 1 + SUBLANES - 1) // SUBLANES) * SUBLANES
    seq_of = lambda b, p: (b * n_pairs + p) * pages_per_step // pages_per_seq
    col_spec = pl.BlockSpec((1, d_attn, 1), lambda b, p, pt: (seq_of(b, p), 0, 0))
    kern = functools.partial(_attn_kernel, nb=nb, n_grid=n_grid, waves_per_step=waves_per_step,
                             pages_per_wave=pages_per_wave, pages_per_seq=pages_per_seq, n_heads=n_heads,
                             past_len=past_len)
    grid_spec = pltpu.PrefetchScalarGridSpec(
        num_scalar_prefetch=1,
        grid=(bsz, n_pairs),
        in_specs=[
            pl.BlockSpec((1, pw, seq), lambda b, p, pt: (b, p, 0)),
            pl.BlockSpec((seq, pw), lambda b, p, pt: (b, p)),
            pl.BlockSpec((seq, pw), lambda b, p, pt: (0, 0)),
            pl.BlockSpec((1, pw, seq), lambda b, p, pt: (b, p, 0)),
            pl.BlockSpec((1, nb, pw), lambda b, p, pt: (b, 0, p)),
            pl.BlockSpec(memory_space=pl.ANY),
            col_spec, col_spec,
        ],
        out_specs=[
            pl.BlockSpec((seq, pw), lambda b, p, pt: (b, p)),
            pl.BlockSpec((1, SUBLANES, LANES), lambda b, p, pt: (seq_of(b, p), 0, 0)),
        ],
        scratch_shapes=[
            pltpu.VMEM((seq, 2 * pw), BF16),
            pltpu.VMEM((pw + 2 * DENOM_ROWS, seq), BF16),
            pltpu.VMEM((waves_per_step, pages_per_wave, d_attn, LANES), F32),
            pltpu.VMEM((n_heads, gate_rows, LANES), F32),
            pltpu.SemaphoreType.DMA((waves_per_step,)),
        ],
    )
    return pl.pallas_call(
        kern,
        grid_spec=grid_spec,
        out_shape=[jax.ShapeDtypeStruct((bsz * seq, d_attn), BF16),
                   jax.ShapeDtypeStruct((n_seq, SUBLANES, LANES), jnp.int32)],
        compiler_params=pltpu.CompilerParams(
            dimension_semantics=("arbitrary", "arbitrary"), vmem_limit_bytes=48 * MIB),
        name="prompt_moba_attention",
    )(page_table, qt, krm, _key_bias_channels(seq), vt, kmean, kc, qcol, kcol)


def _ffn_kernel(x_ref, att_ref, cm_ref, wo_ref, g1_ref, b1_ref, wup_ref, wfc_ref, wdn_ref, g2_ref, b2_ref,
                y_ref, fst_ref,
                x1_ref, x1b_ref, acc_ref, h_ref, ua_ref, ug_ref, carry_ref,
                *, tm, tiles_per_batch, d_attn, d_ff, cw, alpha):
    t = pl.program_id(0)
    mix = _dot(att_ref[...], wo_ref[0:d_attn, :]) + _dot(cm_ref[...], wo_ref[d_attn:, :])
    x1 = _layer_norm(alpha * x_ref[...] + mix, g1_ref[...], b1_ref[...])
    x1_ref[...] = x1
    x1b_ref[...] = x1.astype(BF16)

    @pl.when(t % tiles_per_batch == 0)
    def _():
        carry_ref[...] = jnp.zeros(carry_ref.shape, F32)

    def up_project(col0, ubuf):
        ubuf[0:SUBLANES, :] = carry_ref[:, col0:col0 + cw]
        ubuf[SUBLANES:SUBLANES + tm, :] = _dot(x1b_ref[...], wup_ref[:, col0:col0 + cw])

    def token_conv(col0, ubuf):
        w = wfc_ref[:, col0:col0 + cw]
        conv = (w[0:1, :] * ubuf[SUBLANES - 2:SUBLANES - 2 + tm, :]
                + w[1:2, :] * ubuf[SUBLANES - 1:SUBLANES - 1 + tm, :]
                + w[2:3, :] * ubuf[SUBLANES:SUBLANES + tm, :])
        carry_ref[:, col0:col0 + cw] = ubuf[tm:tm + SUBLANES, :]
        fst_ref[0, :, col0:col0 + cw] = ubuf[SUBLANES + tm - (CONV_W - 1):SUBLANES + tm, :]
        return conv

    n_chunks = d_ff // cw
    for c in range(-1, n_chunks + DOWN_LAG):
        if 0 <= c + 1 < n_chunks:
            up_project((c + 1) * cw, ua_ref.at[(c + 1) % 2])
        if 0 <= c < n_chunks:
            a = _silu(token_conv(c * cw, ua_ref.at[c % 2]))
        if 0 <= c + 1 < n_chunks:
            up_project(d_ff + (c + 1) * cw, ug_ref.at[(c + 1) % 2])
        if 0 <= c < n_chunks:
            h_ref[:, c * cw:(c + 1) * cw] = (a * token_conv(d_ff + c * cw, ug_ref.at[c % 2])).astype(BF16)
        if c >= DOWN_LAG:
            k = c - DOWN_LAG
            part = _dot(h_ref[:, k * cw:(k + 1) * cw], wdn_ref[k * cw:(k + 1) * cw, :])
            if k == 0:
                acc_ref[...] = part
            else:
                acc_ref[...] += part

    y_ref[...] = _layer_norm(alpha * x1_ref[...] + acc_ref[...], g2_ref[...], b2_ref[...])


def _prompt_ffn(x2d, att, cm, wo, g1, b1, wup, wfc, wdn, g2, b2, *, bsz, seq, tm, cw, alpha):
    n, d_model = x2d.shape
    d_attn = att.shape[1]
    d_ff = wdn.shape[0]
    tpb = seq // tm
    kern = functools.partial(_ffn_kernel, tm=tm, tiles_per_batch=tpb, d_attn=d_attn, d_ff=d_ff, cw=cw,
                             alpha=alpha)
    const = lambda t: (0, 0)
    resident = lambda a: pl.BlockSpec(a.shape, const, pipeline_mode=pl.Buffered(1))
    row = lambda w: pl.BlockSpec((tm, w), lambda t: (t, 0))
    return pl.pallas_call(
        kern,
        grid=(n // tm,),
        in_specs=[row(d_model), row(d_attn), row(cm.shape[1]),
                  resident(wo), resident(g1), resident(b1), resident(wup), resident(wfc), resident(wdn),
                  resident(g2), resident(b2)],
        out_specs=[row(d_model),
                   pl.BlockSpec((1, CONV_W - 1, 2 * d_ff), lambda t: (t // tpb, 0, 0))],
        out_shape=[jax.ShapeDtypeStruct((n, d_model), F32),
                   jax.ShapeDtypeStruct((bsz, CONV_W - 1, 2 * d_ff), F32)],
        scratch_shapes=[
            pltpu.VMEM((tm, d_model), F32),
            pltpu.VMEM((tm, d_model), BF16),
            pltpu.VMEM((tm, d_model), F32),
            pltpu.VMEM((tm, d_ff), BF16),
            pltpu.VMEM((2, tm + 2 * SUBLANES, cw), F32),
            pltpu.VMEM((2, tm + 2 * SUBLANES, cw), F32),
            pltpu.VMEM((SUBLANES, 2 * d_ff), F32),
        ],
        compiler_params=pltpu.CompilerParams(
            dimension_semantics=("arbitrary",), vmem_limit_bytes=56 * MIB),
        name="prompt_outproj_ffn",
    )(x2d, att, cm, wo, g1, b1, wup, wfc, wdn, g2, b2)


def _sample_inproj_kernel(x_ref, w_ref, wc_ref, st0_ref, st1_ref,
                          q_ref, k_ref, v_ref, cm_ref, u_ref, *, d_attn, d_conv):
    z = jnp.dot(x_ref[...], w_ref[...], preferred_element_type=F32,
                precision=lax.Precision.HIGHEST)
    q_ref[...] = z[:, 0:d_attn] * 0.125
    k_ref[...] = z[:, d_attn:2 * d_attn]
    v_ref[...] = z[:, 2 * d_attn:3 * d_attn]
    o = 3 * d_attn
    gb = z[:, o:o + d_conv]
    u = z[:, o + d_conv:o + 2 * d_conv] * z[:, o + 2 * d_conv:o + 3 * d_conv]
    wc = wc_ref[...]
    conv = wc[0:1, :] * st0_ref[...] + wc[1:2, :] * st1_ref[...] + wc[2:3, :] * u
    cm_ref[...] = gb * conv
    u_ref[...] = u


def _sample_inproj(xs, w_in_f32, w_conv, st0, st1, *, d_attn, d_conv):
    b = xs.shape[0]
    kern = functools.partial(_sample_inproj_kernel, d_attn=d_attn, d_conv=d_conv)
    sd = lambda w: jax.ShapeDtypeStruct((b, w), F32)
    return pl.pallas_call(
        kern,
        out_shape=[sd(d_attn), sd(d_attn), sd(d_attn), sd(d_conv), sd(d_conv)],
        compiler_params=pltpu.CompilerParams(vmem_limit_bytes=48 * MIB),
        name="sample_inproj",
    )(xs, w_in_f32, w_conv, st0, st1)


def _sel_copies(sel_ref, pt_ref, kc_ref, vc_ref, kbuf_ref, vbuf_ref, sem_ref, g, slot, *, n_heads, group):
    pages_per_block = MOBA_BLOCK // LANES
    copies = []
    for j in range(group):
        b = g * group + j
        for h in range(n_heads):
            for r in range(MOBA_TOPK):
                blk = sel_ref[(b * n_heads + h) * MOBA_TOPK + r]
                for pg in range(pages_per_block):
                    phys = pt_ref[b, blk * pages_per_block + pg]
                    rows = pl.ds(h * HEAD_DIM, HEAD_DIM)
                    t = r * pages_per_block + pg
                    copies.append(pltpu.make_async_copy(kc_ref.at[phys, rows], kbuf_ref.at[slot, j, h, t],
                                                        sem_ref.at[0, slot]))
                    copies.append(pltpu.make_async_copy(vc_ref.at[phys, rows], vbuf_ref.at[slot, j, h, t],
                                                        sem_ref.at[1, slot]))
    return copies


def _sample_attn_kernel(sel_ref, pt_ref, kc_ref, vc_ref, qcol_ref, kcol_ref, vcol_ref, o_ref,
                        kbuf_ref, vbuf_ref, sem_ref, *, n_steps, n_heads, group, past_len):
    g = pl.program_id(0)
    slot = g % 2
    pages_per_block = MOBA_BLOCK // LANES
    n_tiles = MOBA_TOPK * pages_per_block
    copies = functools.partial(_sel_copies, sel_ref, pt_ref, kc_ref, vc_ref, kbuf_ref, vbuf_ref, sem_ref,
                               n_heads=n_heads, group=group)

    @pl.when(g == 0)
    def _():
        for cp in copies(g, slot):
            cp.start()

    @pl.when(g + 1 < n_steps)
    def _():
        for cp in copies(g + 1, 1 - slot):
            cp.start()

    for cp in copies(g, slot):
        cp.wait()

    lane = lax.broadcasted_iota(jnp.int32, (1, LANES), 1)
    chains = [(j, h) for j in range(group) for h in range(n_heads)]
    hs = [slice(h * HEAD_DIM, (h + 1) * HEAD_DIM) for h in range(n_heads)]
    qcol = [qcol_ref[j] for j in range(group)]
    kcol = [kcol_ref[j] for j in range(group)]
    vcol = [vcol_ref[j] for j in range(group)]

    s_own, s_all = {}, {}
    for j, h in chains:
        b = g * group + j
        qb = jnp.broadcast_to(qcol[j][hs[h], :], (HEAD_DIM, LANES))
        s_own[j, h] = jnp.sum(qcol[j][hs[h], :] * kcol[j][hs[h], :], axis=0, keepdims=True)
        rows = []
        for t in range(n_tiles):
            blk = sel_ref[(b * n_heads + h) * MOBA_TOPK + t // pages_per_block]
            pos0 = blk * MOBA_BLOCK + (t % pages_per_block) * LANES
            dist = (past_len - pos0 - lane).astype(F32)
            rows.append(jnp.sum(qb * kbuf_ref[slot, j, h, t], axis=0, keepdims=True) - 2.0 ** -(h + 1) * dist)
        s_all[j, h] = jnp.concatenate(rows, axis=0)
    m = {c: jnp.maximum(s_own[c], jnp.max(s_all[c], axis=(0, 1), keepdims=True)) for c in chains}
    pr = {c: jnp.exp(s_all[c] - m[c]) for c in chains}
    p_own = {c: jnp.exp(s_own[c] - m[c]) for c in chains}
    l = {c: p_own[c] + jnp.sum(pr[c], axis=(0, 1), keepdims=True) for c in chains}
    for j, h in chains:
        acc = p_own[j, h] * vcol[j][hs[h], :]
        wv = None
        for t in range(n_tiles):
            part = vbuf_ref[slot, j, h, t] * pr[j, h][t:t + 1, :]
            wv = part if wv is None else wv + part
        acc = acc + jnp.sum(wv, axis=1, keepdims=True)
        o_ref[j, hs[h], :] = acc * (1.0 / l[j, h])


def _sample_attention(sel_flat, page_table, kc, vc, qcol, kcol, vcol, *, n_heads, past_len):
    bsz = page_table.shape[0]
    d_attn = n_heads * HEAD_DIM
    n_tiles = MOBA_TOPK * (MOBA_BLOCK // LANES)
    group = next(g for g in (4, 2, 1) if bsz % g == 0)
    n_steps = bsz // group
    kern = functools.partial(_sample_attn_kernel, n_steps=n_steps, n_heads=n_heads, group=group,
                             past_len=past_len)
    col_spec = pl.BlockSpec((group, d_attn, 1), lambda g, sel, pt: (g, 0, 0))
    any_spec = pl.BlockSpec(memory_space=pl.ANY)
    grid_spec = pltpu.PrefetchScalarGridSpec(
        num_scalar_prefetch=2,
        grid=(n_steps,),
        in_specs=[any_spec, any_spec, col_spec, col_spec, col_spec],
        out_specs=col_spec,
        scratch_shapes=[
            pltpu.VMEM((2, group, n_heads, n_tiles, HEAD_DIM, LANES), F32),
            pltpu.VMEM((2, group, n_heads, n_tiles, HEAD_DIM, LANES), F32),
            pltpu.SemaphoreType.DMA((2, 2)),
        ],
    )
    return pl.pallas_call(
        kern,
        grid_spec=grid_spec,
        out_shape=jax.ShapeDtypeStruct((bsz, d_attn, 1), F32),
        compiler_params=pltpu.CompilerParams(
            dimension_semantics=("arbitrary",), vmem_limit_bytes=48 * MIB),
        name="sample_moba_attention",
    )(sel_flat, page_table, kc, vc, qcol, kcol, vcol)


def _sample_ffn_kernel(x_ref, att_ref, cm_ref, wo_ref, g1_ref, b1_ref, wup_ref, wfc_ref, sf0_ref, sf1_ref,
                       wdn_ref, g2_ref, b2_ref, y_ref, up_ref, *, d_attn, d_ff, alpha):
    mix = (_dot(att_ref[...].astype(BF16), wo_ref[0:d_attn, :])
           + _dot(cm_ref[...].astype(BF16), wo_ref[d_attn:, :]))
    x1 = _layer_norm(alpha * x_ref[...] + mix, g1_ref[...], b1_ref[...])
    up = _dot(x1.astype(BF16), wup_ref[...])
    up_ref[...] = up
    w = wfc_ref[...]
    conv = w[0:1, :] * sf0_ref[...] + w[1:2, :] * sf1_ref[...] + w[2:3, :] * up
    hcol = (_silu(conv[:, 0:d_ff]) * conv[:, d_ff:]).astype(BF16)
    f = _dot(hcol, wdn_ref[...])
    y_ref[...] = _layer_norm(alpha * x1 + f, g2_ref[...], b2_ref[...])


def _sample_ffn(xs, att, cm, wo, g1, b1, wup, wfc, sf0, sf1, wdn, g2, b2, *, alpha):
    b, d_model = xs.shape
    d_ff = wdn.shape[0]
    kern = functools.partial(_sample_ffn_kernel, d_attn=att.shape[1], d_ff=d_ff, alpha=alpha)
    return pl.pallas_call(
        kern,
        out_shape=[jax.ShapeDtypeStruct((b, d_model), F32), jax.ShapeDtypeStruct((b, 2 * d_ff), F32)],
        compiler_params=pltpu.CompilerParams(vmem_limit_bytes=48 * MIB),
        name="sample_outproj_ffn",
    )(xs, att, cm, wo, g1, b1, wup, wfc, sf0, sf1, wdn, g2, b2)


def kernel(x_prompt, x_sample, cache_k, cache_v, state_conv, state_ffn_conv, page_table, w_in, w_conv, w_out,
           ln1_g, ln1_b, w_ffn_up, w_ffn_conv, w_ffn_down, ln2_g, ln2_b):
    depth = w_in.shape[0]
    assert depth == 1, "single-layer step"
    bsz, seq, d_model = x_prompt.shape
    dec_b, dec_t, _ = x_sample.shape
    assert dec_t == 1
    _, n_phys, page_size, n_heads, head_dim = cache_k.shape
    assert head_dim == HEAD_DIM and page_size == LANES
    d_attn = n_heads * head_dim
    d_conv = (w_in.shape[2] - 3 * d_attn) // 3
    d_ff = w_ffn_down.shape[1]
    past_len = page_table.shape[1] * page_size
    alpha = (2.0 * depth) ** 0.25
    row = lambda a: a.reshape(1, -1)

    w_in_b = w_in[0].astype(BF16)
    wt = w_in_b[:, :3 * d_attn].T
    wr = w_in_b[:, d_attn:2 * d_attn]
    wr = jnp.concatenate([wr, w_in_b[:, 3 * d_attn:]], axis=1)
    wo = w_out[0].astype(BF16)
    wup = w_ffn_up[0].astype(BF16)
    wdn = w_ffn_down[0].astype(BF16)
    wc, wfc = w_conv[0], w_ffn_conv[0]
    g1, b1, g2, b2 = row(ln1_g[0]), row(ln1_b[0]), row(ln2_g[0]), row(ln2_b[0])

    xs = x_sample.reshape(dec_b, d_model)
    st0, st1 = state_conv[0, :, 0, :], state_conv[0, :, 1, :]
    q_s, k_s, v_s, cm_s, u_s = _sample_inproj(xs, w_in[0], wc, st0, st1, d_attn=d_attn, d_conv=d_conv)
    kc = jnp.transpose(cache_k[0], (0, 2, 3, 1)).reshape(n_phys, d_attn, page_size)
    vc = jnp.transpose(cache_v[0], (0, 2, 3, 1)).reshape(n_phys, d_attn, page_size)
    qcol, kcol, vcol = q_s[:, :, None], k_s[:, :, None], v_s[:, :, None]

    x2d = x_prompt.reshape(bsz * seq, d_model)
    qt, kt, vt, krm, kmean, cm, conv_p = _prompt_inproj(x2d, wt, wr, wc, bsz=bsz, seq=seq, tm=1024)
    att, sel = _prompt_attention_and_sample_gate(qt, krm, vt, kmean, page_table, kc, qcol, kcol,
                                                 bsz=bsz, seq=seq, past_len=past_len, pages_per_wave=32)
    y_p, ffn_p = _prompt_ffn(x2d, att, cm, wo, g1, b1, wup, wfc, wdn, g2, b2,
                             bsz=bsz, seq=seq, tm=512, cw=256, alpha=alpha)
    to_cache = lambda a: jnp.transpose(a.reshape(1, bsz, n_heads, head_dim, seq), (0, 1, 4, 2, 3))
    k_prompt, v_prompt = to_cache(kt), to_cache(vt)

    sel_flat = jnp.transpose(sel[:, :MOBA_TOPK, :n_heads], (0, 2, 1)).reshape(-1)
    att_s = _sample_attention(sel_flat, page_table, kc, vc, qcol, kcol, vcol,
                              n_heads=n_heads, past_len=past_len)
    sf0, sf1 = state_ffn_conv[0, :, 0, :], state_ffn_conv[0, :, 1, :]
    y_s, up_s = _sample_ffn(xs, att_s.reshape(dec_b, d_attn), cm_s, wo, g1, b1, wup, wfc, sf0, sf1,
                            wdn, g2, b2, alpha=alpha)

    return (y_p.reshape(bsz, seq, d_model),
            y_s.reshape(dec_b, 1, d_model),
            k_prompt, v_prompt,
            conv_p[None], ffn_p[None],
            k_s.reshape(1, dec_b, 1, n_heads, head_dim), v_s.reshape(1, dec_b, 1, n_heads, head_dim),
            jnp.stack([st1, u_s], axis=1)[None],
            jnp.stack([sf1, up_s], axis=1)[None])
```

```python
import functools

import jax
import jax.numpy as jnp
from jax import lax
from jax.experimental import pallas as pl
from jax.experimental.pallas import tpu as pltpu

F32 = jnp.float32
BF16 = jnp.bfloat16

HEAD_DIM = 64
MOBA_BLOCK = 256
MOBA_TOPK = 3
CONV_W = 3
LN_EPS = 1e-5
LOG2E = 1.4426950408889634
NEG = -1e30
SUBLANES = 8
LANES = 128
MIB = 1024 * 1024

_NT = (((1,), (1,)), ((), ()))


def _dot(a, b):
    return jnp.dot(a, b, preferred_element_type=F32)


def _dot_nt(a, b):
    return lax.dot_general(a, b, _NT, preferred_element_type=F32)


def _layer_norm(x, g, b):
    mu = jnp.mean(x, axis=-1, keepdims=True)
    xc = x - mu
    var = jnp.mean(xc * xc, axis=-1, keepdims=True)
    return xc * lax.rsqrt(var + LN_EPS) * g + b


def _silu(a):
    return a * jax.nn.sigmoid(a)


def _inproj_kernel(x_ref, wt_ref, wr_ref, wc_ref,
                   qt_ref, kt_ref, vt_ref, krm_ref, kmean_ref, cm_ref, cst_ref,
                   ubuf_ref, *, tm, tiles_per_batch, d_attn, d_conv):
    t = pl.program_id(0)
    tb = t % tiles_per_batch
    @pl.when(tb == 0)
    def _():
        ubuf_ref[0:SUBLANES, :] = jnp.zeros((SUBLANES, d_conv), F32)

    xb = x_ref[...].astype(BF16)

    gc = _dot(xb, wr_ref[:, d_attn + d_conv:d_attn + 2 * d_conv])
    hh = _dot(xb, wr_ref[:, d_attn + 2 * d_conv:d_attn + 3 * d_conv])
    u = gc * hh
    gb = _dot(xb, wr_ref[:, d_attn:d_attn + d_conv])
    ubuf_ref[SUBLANES:SUBLANES + tm, :] = u
    wc = wc_ref[...]
    conv = (wc[0:1, :] * ubuf_ref[SUBLANES - 2:SUBLANES - 2 + tm, :]
            + wc[1:2, :] * ubuf_ref[SUBLANES - 1:SUBLANES - 1 + tm, :]
            + wc[2:3, :] * u)
    cm_ref[...] = (gb * conv).astype(BF16)
    cst_ref[0] = ubuf_ref[SUBLANES + tm - (CONV_W - 1):SUBLANES + tm, :]
    ubuf_ref[0:SUBLANES, :] = ubuf_ref[tm:tm + SUBLANES, :]

    k = _dot(xb, wr_ref[:, 0:d_attn])
    kt_ref[0] = k.T
    krm_ref[...] = k.astype(BF16)
    for blk in range(tm // MOBA_BLOCK):
        ksum = jnp.sum(k[blk * MOBA_BLOCK:(blk + 1) * MOBA_BLOCK, :], axis=0, keepdims=True)
        kmean_ref[0, pl.ds(tb * (tm // MOBA_BLOCK) + blk, 1), :] = ksum * (1.0 / MOBA_BLOCK)

    qt_ref[0] = (_dot_nt(wt_ref[0:d_attn, :], xb) * (LOG2E / HEAD_DIM ** 0.5)).astype(BF16)
    vt_ref[0] = _dot_nt(wt_ref[2 * d_attn:3 * d_attn, :], xb)


def _prompt_inproj(x2d, wt, wr, w_conv, *, bsz, seq, tm):
    n, d_model = x2d.shape
    d_attn = wt.shape[0] // 3
    d_conv = (wr.shape[1] - d_attn) // 3
    tpb = seq // tm
    nb = seq // MOBA_BLOCK
    kern = functools.partial(_inproj_kernel, tm=tm, tiles_per_batch=tpb, d_attn=d_attn, d_conv=d_conv)
    const = lambda t: (0, 0)
    ct_spec = lambda: pl.BlockSpec((1, d_attn, tm), lambda t: (t // tpb, 0, t % tpb))
    return pl.pallas_call(
        kern,
        grid=(n // tm,),
        in_specs=[
            pl.BlockSpec((tm, d_model), lambda t: (t, 0)),
            pl.BlockSpec(wt.shape, const),
            pl.BlockSpec(wr.shape, const),
            pl.BlockSpec(w_conv.shape, const),
        ],
        out_specs=[
            ct_spec(), ct_spec(), ct_spec(),
            pl.BlockSpec((tm, d_attn), lambda t: (t, 0)),
            pl.BlockSpec((1, nb, d_attn), lambda t: (t // tpb, 0, 0)),
            pl.BlockSpec((tm, d_conv), lambda t: (t, 0)),
            pl.BlockSpec((1, CONV_W - 1, d_conv), lambda t: (t // tpb, 0, 0)),
        ],
        out_shape=[
            jax.ShapeDtypeStruct((bsz, d_attn, seq), BF16),
            jax.ShapeDtypeStruct((bsz, d_attn, seq), F32),
            jax.ShapeDtypeStruct((bsz, d_attn, seq), F32),
            jax.ShapeDtypeStruct((n, d_attn), BF16),
            jax.ShapeDtypeStruct((bsz, nb, d_attn), F32),
            jax.ShapeDtypeStruct((n, d_conv), BF16),
            jax.ShapeDtypeStruct((bsz, CONV_W - 1, d_conv), F32),
        ],
        scratch_shapes=[pltpu.VMEM((tm + 2 * SUBLANES, d_conv), F32)],
        compiler_params=pltpu.CompilerParams(
            dimension_semantics=("arbitrary",), vmem_limit_bytes=56 * MIB),
        name="prompt_inproj",
    )(x2d, wt, wr, w_conv)


BIAS_CH0 = 16
GATE_PREFETCH = 1
DOWN_LAG = 7
DENOM_ROWS = 16


def _key_bias_channels(seq):
    pos = jnp.arange(seq, dtype=jnp.int32)[:, None]
    c = jnp.arange(2 * HEAD_DIM, dtype=jnp.int32)[None, :] - BIAS_CH0
    kblk, koff = pos // MOBA_BLOCK, pos % MOBA_BLOCK
    ext = jnp.where(c + BIAS_CH0 < SUBLANES, (kblk == c + BIAS_CH0).astype(F32), 0.0)
    ext = jnp.where((c >= 0) & (c < 3), koff.astype(F32), ext)
    ext = jnp.where((c >= 3) & (c < 6), (kblk * MOBA_BLOCK).astype(F32), ext)
    ext = jnp.where((c >= 6) & (c < 12), 1.0, ext)
    return ext.astype(BF16)


def _query_alibi_channels(coef, block, nq):
    row = lax.broadcasted_iota(jnp.int32, (2 * SUBLANES, nq), 0)
    qoff = lax.broadcasted_iota(jnp.int32, (2 * SUBLANES, nq), 1).astype(F32)
    cb = jnp.full((2 * SUBLANES, nq), 1.0, F32) * coef
    val = jnp.where(row < 6, cb,
                    jnp.where(row < 9, -cb * qoff,
                              jnp.where(row < 12, -cb * float(block * MOBA_BLOCK), 0.0)))
    hi = val.astype(BF16).astype(F32)
    mid = (val - hi).astype(BF16).astype(F32)
    lo = ((val - hi) - mid).astype(BF16).astype(F32)
    part = row % 3
    return jnp.where(part == 0, hi, jnp.where(part == 1, mid, lo))


def _selection_bias(g, n_past):
    sub = lax.broadcasted_iota(jnp.int32, g.shape, 0)
    rank = jnp.zeros(g.shape, jnp.int32)
    for jp in range(n_past):
        row = g[jp:jp + 1, :]
        beats = (row > g) | ((row == g) & (jp < sub))
        rank = rank + jnp.where(beats, 1, 0)
    sel = ((sub < n_past) & (rank < MOBA_TOPK)) | (sub == n_past)
    return jnp.where(sel, 0.0, NEG).astype(F32)


def _gate_wave_copies(pt_ref, kc_ref, buf_ref, sem_ref, wave, slot, *, pages_per_wave, pages_per_seq):
    first = wave * pages_per_wave
    seq_idx = first // pages_per_seq
    pg0 = first % pages_per_seq
    return [pltpu.make_async_copy(kc_ref.at[pt_ref[seq_idx, pg0 + pg]], buf_ref.at[slot, pg], sem_ref.at[slot])
            for pg in range(pages_per_wave)]


def _sample_gate_wave(copies, wave, w, part, n_parts, n_waves, qcol_ref, buf_ref, g_ref,
                      *, pages_per_wave, pages_per_seq, n_heads):
    slot = w
    n_slots = buf_ref.shape[0]
    pages_per_block = MOBA_BLOCK // LANES
    blocks_per_part = pages_per_wave // pages_per_block // n_parts
    n_blocks = pages_per_seq // pages_per_block
    row0 = (wave * pages_per_wave % pages_per_seq) // pages_per_block

    if part == 0:
        for cp in copies(jnp.minimum(wave + GATE_PREFETCH, n_waves - 1), (w + GATE_PREFETCH) % n_slots):
            cp.start()
        for cp in copies(wave, slot):
            cp.wait()
        g_ref[:, n_blocks:, :] = jnp.zeros((n_heads, g_ref.shape[1] - n_blocks, LANES), F32)

    qcol = qcol_ref[0]
    for h in range(n_heads):
        qb = jnp.broadcast_to(qcol[h * HEAD_DIM:(h + 1) * HEAD_DIM, :], (HEAD_DIM, LANES))
        for bl in range(part * blocks_per_part, (part + 1) * blocks_per_part):
            ksum = None
            for pg in range(pages_per_block):
                kt = buf_ref[slot, bl * pages_per_block + pg, h * HEAD_DIM:(h + 1) * HEAD_DIM, :]
                ksum = kt if ksum is None else ksum + kt
            g_ref[h, pl.ds(row0 + bl, 1), :] = jnp.sum(qb * ksum, axis=0, keepdims=True)


def _sample_gate_topk(qcol_ref, kcol_ref, sel_ref, g_ref, *, n_heads, past_len):
    qcol = qcol_ref[0]
    kcol = kcol_ref[0]
    own = past_len // MOBA_BLOCK
    nrow = g_ref.shape[1]
    lane = lax.broadcasted_iota(jnp.int32, (nrow, LANES), 1)
    sub = lax.broadcasted_iota(jnp.int32, (nrow, LANES), 0)
    gate = jnp.full((nrow, LANES), -jnp.inf, F32)
    for h in range(n_heads):
        tot = jnp.sum(g_ref[h], axis=1, keepdims=True)
        own_dot = jnp.sum(qcol[h * HEAD_DIM:(h + 1) * HEAD_DIM, :] * kcol[h * HEAD_DIM:(h + 1) * HEAD_DIM, :],
                          axis=0, keepdims=True)
        tot = jnp.where(sub[:, 0:1] == own, own_dot, tot) * (1.0 / MOBA_BLOCK)
        gate = jnp.where(lane == h, tot, gate)
    gate = jnp.where(sub < own, gate, -jnp.inf)
    rows = []
    for _ in range(MOBA_TOPK):
        m = jnp.max(gate, axis=0, keepdims=True)
        idx = jnp.min(jnp.where(gate == m, sub, nrow), axis=0, keepdims=True)
        rows.append(idx)
        gate = jnp.where(sub == idx, -jnp.inf, gate)
    rows.append(jnp.zeros((SUBLANES - MOBA_TOPK, LANES), jnp.int32))
    sel_ref[0] = jnp.concatenate(rows, axis=0)


def _attn_kernel(pt_ref, qt_ref, k_ref, kx_ref, vt_ref, kmean_ref, kc_ref, qcol_ref, kcol_ref,
                 o_ref, sel_ref,
                 kk_ref, vtb_ref, pbuf_ref, g_ref, sem_ref,
                 *, nb, n_grid, waves_per_step, pages_per_wave, pages_per_seq, n_heads, past_len):
    pair = pl.program_id(1)
    step = pl.program_id(0) * pl.num_programs(1) + pair
    blk = MOBA_BLOCK
    pw = 2 * HEAD_DIM

    n_waves = n_grid * waves_per_step
    copies = functools.partial(_gate_wave_copies, pt_ref, kc_ref, pbuf_ref, sem_ref,
                               pages_per_wave=pages_per_wave, pages_per_seq=pages_per_seq)
    gate_wave = functools.partial(_sample_gate_wave, copies, qcol_ref=qcol_ref, buf_ref=pbuf_ref, g_ref=g_ref,
                                  n_waves=n_waves, pages_per_wave=pages_per_wave, pages_per_seq=pages_per_seq,
                                  n_heads=n_heads)

    @pl.when(step == 0)
    def _():
        for ahead in range(GATE_PREFETCH):
            for cp in copies(ahead, ahead):
                cp.start()

    kk_ref[:, 0:pw] = k_ref[...]
    kk_ref[:, pw:2 * pw] = kx_ref[...]
    vrows = HEAD_DIM + DENOM_ROWS
    ones_row = jnp.where(lax.broadcasted_iota(jnp.int32, (DENOM_ROWS, vt_ref.shape[2]), 0) == 0,
                         1.0, 0.0).astype(BF16)
    for e in range(2):
        vtb_ref[e * vrows:e * vrows + HEAD_DIM, :] = vt_ref[0, e * HEAD_DIM:(e + 1) * HEAD_DIM, :].astype(BF16)
        vtb_ref[e * vrows + HEAD_DIM:(e + 1) * vrows, :] = ones_row

    slope_even = jnp.where(pair == 0, 2.0 ** -1,
                           jnp.where(pair == 1, 2.0 ** -3,
                                     jnp.where(pair == 2, 2.0 ** -5, 2.0 ** -7))).astype(F32)
    causal = (lax.broadcasted_iota(jnp.int32, (blk, blk), 1)
              >= lax.broadcasted_iota(jnp.int32, (blk, blk), 0))
    zeros8 = jnp.zeros((SUBLANES, blk), F32)
    zeros_h = jnp.zeros((HEAD_DIM, blk), BF16)
    zeros_tail = jnp.zeros((pw - 2 * BIAS_CH0, blk), BF16)
    kmean = kmean_ref[0]
    km_hi = kmean.astype(BF16)
    km_mid = (kmean - km_hi.astype(F32)).astype(BF16)
    km_lo = ((kmean - km_hi.astype(F32)) - km_mid.astype(F32)).astype(BF16)

    def scores(i, e):
        sl = slope_even * (0.5 ** e)
        qh = qt_ref[0, e * HEAD_DIM:(e + 1) * HEAD_DIM, i * blk:(i + 1) * blk]
        if i > 0:
            cols = slice(e * HEAD_DIM, (e + 1) * HEAD_DIM)
            gate = _dot(km_hi[:, cols], qh) + _dot(km_mid[:, cols], qh) + _dot(km_lo[:, cols], qh)
        else:
            gate = zeros8
        selb = _selection_bias(gate, i)
        q_ext = jnp.concatenate([qh, zeros_h] if e == 0 else [zeros_h, qh], axis=0)
        q_ext = jnp.concatenate(
            [q_ext,
             jnp.concatenate([selb, zeros8], axis=0).astype(BF16),
             _query_alibi_channels(sl * LOG2E, i, blk).astype(BF16),
             zeros_tail], axis=0)
        s = _dot(kk_ref[0:(i + 1) * blk, :], q_ext)
        s_own = jnp.where(causal, s[i * blk:, :], NEG)
        return s_own if i == 0 else jnp.concatenate([s[0:i * blk, :], s_own], axis=0)

    def softmax(s):
        m = jnp.max(s, axis=0, keepdims=True)
        return (jnp.exp2(s - m).astype(BF16),)

    def weighted_values(i, e, pr):
        acc = _dot(vtb_ref[e * vrows:(e + 1) * vrows, 0:(i + 1) * blk], pr)
        return acc[0:HEAD_DIM, :] * (1.0 / acc[HEAD_DIM:HEAD_DIM + 1, :])

    order = [(i, e) for i in range(nb) for e in range(2)]
    n_steps = len(order)
    s_buf, p_buf, outs = {}, {}, []
    parts_per_wave = n_steps // waves_per_step
    work = [i + 1 for i, _ in order]
    gate_parts_at = {n: [] for n in range(n_steps)}
    for k in range(n_steps):
        target = (k + 0.5) * sum(work) / n_steps
        gate_parts_at[next(n for n in range(n_steps) if sum(work[:n + 1]) >= target)].append(k)
    for n in range(-1, n_steps + 1):
        for k in gate_parts_at.get(n + 1, ()):
            w, part = divmod(k, parts_per_wave)
            gate_wave(step * waves_per_step + w, w, part, parts_per_wave)
        if 0 <= n + 1 < n_steps:
            s_buf[n + 1] = scores(*order[n + 1])
        if 0 <= n < n_steps:
            p_buf[n] = softmax(s_buf.pop(n))
        if n >= 1:
            i, e = order[n - 1]
            outs.append(weighted_values(i, e, *p_buf.pop(n - 1)))
            if e == 1:
                o_ref[i * blk:(i + 1) * blk, :] = jnp.concatenate(outs, axis=0).T.astype(BF16)
                outs = []

    @pl.when((step + 1) * waves_per_step * pages_per_wave % pages_per_seq == 0)
    def _():
        _sample_gate_topk(qcol_ref, kcol_ref, sel_ref, g_ref, n_heads=n_heads, past_len=past_len)

    @pl.when(step == n_grid - 1)
    def _():
        for ahead in range(GATE_PREFETCH):
            for cp in copies(n_waves - 1, ahead):
                cp.wait()


def _prompt_attention_and_sample_gate(qt, krm, vt, kmean, page_table, kc, qcol, kcol,
                                      *, bsz, seq, past_len, pages_per_wave):
    d_attn = qt.shape[1]
    nb = seq // MOBA_BLOCK
    assert nb <= SUBLANES, "selection-bias channels hold one block per row of a vector register"
    n_pairs = d_attn // (2 * HEAD_DIM)
    n_heads = d_attn // HEAD_DIM
    pw = 2 * HEAD_DIM
    n_grid = bsz * n_pairs
    n_seq, pages_per_seq = page_table.shape
    pages_per_step = n_seq * pages_per_seq // n_grid
    assert pages_per_step * n_grid == n_seq * pages_per_seq and pages_per_seq % pages_per_step == 0
    waves_per_step = pages_per_step // pages_per_wave
    assert waves_per_step * pages_per_wave == pages_per_step and waves_per_step > GATE_PREFETCH
    assert (2 * nb) % waves_per_step == 0
    assert pages_per_wave % ((MOBA_BLOCK // LANES) * (2 * nb // waves_per_step)) == 0
    gate_rows = ((pages_per_seq * LANES // MOBA_BLOCK + 1 + SUBLANES - 1) // SUBLANES) * SUBLANES
    seq_of = lambda b, p: (b * n_pairs + p) * pages_per_step // pages_per_seq
    col_spec = pl.BlockSpec((1, d_attn, 1), lambda b, p, pt: (seq_of(b, p), 0, 0))
    kern = functools.partial(_attn_kernel, nb=nb, n_grid=n_grid, waves_per_step=waves_per_step,
                             pages_per_wave=pages_per_wave, pages_per_seq=pages_per_seq, n_heads=n_heads,
                             past_len=past_len)
    grid_spec = pltpu.PrefetchScalarGridSpec(
        num_scalar_prefetch=1,
        grid=(bsz, n_pairs),
        in_specs=[
            pl.BlockSpec((1, pw, seq), lambda b, p, pt: (b, p, 0)),
            pl.BlockSpec((seq, pw), lambda b, p, pt: (b, p)),
            pl.BlockSpec((seq, pw), lambda b, p, pt: (0, 0)),
            pl.BlockSpec((1, pw, seq), lambda b, p, pt: (b, p, 0)),
            pl.BlockSpec((1, nb, pw), lambda b, p, pt: (b, 0, p)),
            pl.BlockSpec(memory_space=pl.ANY),
            col_spec, col_spec,
        ],
        out_specs=[
            pl.BlockSpec((seq, pw), lambda b, p, pt: (b, p)),
            pl.BlockSpec((1, SUBLANES, LANES), lambda b, p, pt: (seq_of(b, p), 0, 0)),
        ],
        scratch_shapes=[
            pltpu.VMEM((seq, 2 * pw), BF16),
            pltpu.VMEM((pw + 2 * DENOM_ROWS, seq), BF16),
            pltpu.VMEM((waves_per_step, pages_per_wave, d_attn, LANES), F32),
            pltpu.VMEM((n_heads, gate_rows, LANES), F32),
            pltpu.SemaphoreType.DMA((waves_per_step,)),
        ],
    )
    return pl.pallas_call(
        kern,
        grid_spec=grid_spec,
        out_shape=[jax.ShapeDtypeStruct((bsz * seq, d_attn), BF16),
                   jax.ShapeDtypeStruct((n_seq, SUBLANES, LANES), jnp.int32)],
        compiler_params=pltpu.CompilerParams(
            dimension_semantics=("arbitrary", "arbitrary"), vmem_limit_bytes=48 * MIB),
        name="prompt_moba_attention",
    )(page_table, qt, krm, _key_bias_channels(seq), vt, kmean, kc, qcol, kcol)


def _ffn_kernel(x_ref, att_ref, cm_ref, wo_ref, g1_ref, b1_ref, wup_ref, wfc_ref, wdn_ref, g2_ref, b2_ref,
                y_ref, fst_ref,
                x1_ref, x1b_ref, acc_ref, h_ref, ua_ref, ug_ref, carry_ref,
                *, tm, tiles_per_batch, d_attn, d_ff, cw, alpha):
    t = pl.program_id(0)
    mix = _dot(att_ref[...], wo_ref[0:d_attn, :]) + _dot(cm_ref[...], wo_ref[d_attn:, :])
    x1 = _layer_norm(alpha * x_ref[...] + mix, g1_ref[...], b1_ref[...])
    x1_ref[...] = x1
    x1b_ref[...] = x1.astype(BF16)

    @pl.when(t % tiles_per_batch == 0)
    def _():
        carry_ref[...] = jnp.zeros(carry_ref.shape, F32)

    def up_project(col0, ubuf):
        ubuf[0:SUBLANES, :] = carry_ref[:, col0:col0 + cw]
        ubuf[SUBLANES:SUBLANES + tm, :] = _dot(x1b_ref[...], wup_ref[:, col0:col0 + cw])

    def token_conv(col0, ubuf):
        w = wfc_ref[:, col0:col0 + cw]
        conv = (w[0:1, :] * ubuf[SUBLANES - 2:SUBLANES - 2 + tm, :]
                + w[1:2, :] * ubuf[SUBLANES - 1:SUBLANES - 1 + tm, :]
                + w[2:3, :] * ubuf[SUBLANES:SUBLANES + tm, :])
        carry_ref[:, col0:col0 + cw] = ubuf[tm:tm + SUBLANES, :]
        fst_ref[0, :, col0:col0 + cw] = ubuf[SUBLANES + tm - (CONV_W - 1):SUBLANES + tm, :]
        return conv

    n_chunks = d_ff // cw
    for c in range(-1, n_chunks + DOWN_LAG):
        if 0 <= c + 1 < n_chunks:
            up_project((c + 1) * cw, ua_ref.at[(c + 1) % 2])
        if 0 <= c < n_chunks:
            a = _silu(token_conv(c * cw, ua_ref.at[c % 2]))
        if 0 <= c + 1 < n_chunks:
            up_project(d_ff + (c + 1) * cw, ug_ref.at[(c + 1) % 2])
        if 0 <= c < n_chunks:
            h_ref[:, c * cw:(c + 1) * cw] = (a * token_conv(d_ff + c * cw, ug_ref.at[c % 2])).astype(BF16)
        if c >= DOWN_LAG:
            k = c - DOWN_LAG
            part = _dot(h_ref[:, k * cw:(k + 1) * cw], wdn_ref[k * cw:(k + 1) * cw, :])
            if k == 0:
                acc_ref[...] = part
            else:
                acc_ref[...] += part

    y_ref[...] = _layer_norm(alpha * x1_ref[...] + acc_ref[...], g2_ref[...], b2_ref[...])


def _prompt_ffn(x2d, att, cm, wo, g1, b1, wup, wfc, wdn, g2, b2, *, bsz, seq, tm, cw, alpha):
    n, d_model = x2d.shape
    d_attn = att.shape[1]
    d_ff = wdn.shape[0]
    tpb = seq // tm
    kern = functools.partial(_ffn_kernel, tm=tm, tiles_per_batch=tpb, d_attn=d_attn, d_ff=d_ff, cw=cw,
                             alpha=alpha)
    const = lambda t: (0, 0)
    resident = lambda a: pl.BlockSpec(a.shape, const, pipeline_mode=pl.Buffered(1))
    row = lambda w: pl.BlockSpec((tm, w), lambda t: (t, 0))
    return pl.pallas_call(
        kern,
        grid=(n // tm,),
        in_specs=[row(d_model), row(d_attn), row(cm.shape[1]),
                  resident(wo), resident(g1), resident(b1), resident(wup), resident(wfc), resident(wdn),
                  resident(g2), resident(b2)],
        out_specs=[row(d_model),
                   pl.BlockSpec((1, CONV_W - 1, 2 * d_ff), lambda t: (t // tpb, 0, 0))],
        out_shape=[jax.ShapeDtypeStruct((n, d_model), F32),
                   jax.ShapeDtypeStruct((bsz, CONV_W - 1, 2 * d_ff), F32)],
        scratch_shapes=[
            pltpu.VMEM((tm, d_model), F32),
            pltpu.VMEM((tm, d_model), BF16),
            pltpu.VMEM((tm, d_model), F32),
            pltpu.VMEM((tm, d_ff), BF16),
            pltpu.VMEM((2, tm + 2 * SUBLANES, cw), F32),
            pltpu.VMEM((2, tm + 2 * SUBLANES, cw), F32),
            pltpu.VMEM((SUBLANES, 2 * d_ff), F32),
        ],
        compiler_params=pltpu.CompilerParams(
            dimension_semantics=("arbitrary",), vmem_limit_bytes=56 * MIB),
        name="prompt_outproj_ffn",
    )(x2d, att, cm, wo, g1, b1, wup, wfc, wdn, g2, b2)


def _sample_inproj_kernel(x_ref, w_ref, wc_ref, st0_ref, st1_ref,
                          q_ref, k_ref, v_ref, cm_ref, u_ref, *, d_attn, d_conv):
    z = jnp.dot(x_ref[...], w_ref[...], preferred_element_type=F32,
                precision=lax.Precision.HIGHEST)
    q_ref[...] = z[:, 0:d_attn] * 0.125
    k_ref[...] = z[:, d_attn:2 * d_attn]
    v_ref[...] = z[:, 2 * d_attn:3 * d_attn]
    o = 3 * d_attn
    gb = z[:, o:o + d_conv]
    u = z[:, o + d_conv:o + 2 * d_conv] * z[:, o + 2 * d_conv:o + 3 * d_conv]
    wc = wc_ref[...]
    conv = wc[0:1, :] * st0_ref[...] + wc[1:2, :] * st1_ref[...] + wc[2:3, :] * u
    cm_ref[...] = gb * conv
    u_ref[...] = u


def _sample_inproj(xs, w_in_f32, w_conv, st0, st1, *, d_attn, d_conv):
    b = xs.shape[0]
    kern = functools.partial(_sample_inproj_kernel, d_attn=d_attn, d_conv=d_conv)
    sd = lambda w: jax.ShapeDtypeStruct((b, w), F32)
    return pl.pallas_call(
        kern,
        out_shape=[sd(d_attn), sd(d_attn), sd(d_attn), sd(d_conv), sd(d_conv)],
        compiler_params=pltpu.CompilerParams(vmem_limit_bytes=48 * MIB),
        name="sample_inproj",
    )(xs, w_in_f32, w_conv, st0, st1)


def _sel_copies(sel_ref, pt_ref, kc_ref, vc_ref, kbuf_ref, vbuf_ref, sem_ref, g, slot, *, n_heads, group):
    pages_per_block = MOBA_BLOCK // LANES
    copies = []
    for j in range(group):
        b = g * group + j
        for h in range(n_heads):
            for r in range(MOBA_TOPK):
                blk = sel_ref[(b * n_heads + h) * MOBA_TOPK + r]
                for pg in range(pages_per_block):
                    phys = pt_ref[b, blk * pages_per_block + pg]
                    rows = pl.ds(h * HEAD_DIM, HEAD_DIM)
                    t = r * pages_per_block + pg
                    copies.append(pltpu.make_async_copy(kc_ref.at[phys, rows], kbuf_ref.at[slot, j, h, t],
                                                        sem_ref.at[0, slot]))
                    copies.append(pltpu.make_async_copy(vc_ref.at[phys, rows], vbuf_ref.at[slot, j, h, t],
                                                        sem_ref.at[1, slot]))
    return copies


def _sample_attn_kernel(sel_ref, pt_ref, kc_ref, vc_ref, qcol_ref, kcol_ref, vcol_ref, o_ref,
                        kbuf_ref, vbuf_ref, sem_ref, *, n_steps, n_heads, group, past_len):
    g = pl.program_id(0)
    slot = g % 2
    pages_per_block = MOBA_BLOCK // LANES
    n_tiles = MOBA_TOPK * pages_per_block
    copies = functools.partial(_sel_copies, sel_ref, pt_ref, kc_ref, vc_ref, kbuf_ref, vbuf_ref, sem_ref,
                               n_heads=n_heads, group=group)

    @pl.when(g == 0)
    def _():
        for cp in copies(g, slot):
            cp.start()

    @pl.when(g + 1 < n_steps)
    def _():
        for cp in copies(g + 1, 1 - slot):
            cp.start()

    for cp in copies(g, slot):
        cp.wait()

    lane = lax.broadcasted_iota(jnp.int32, (1, LANES), 1)
    chains = [(j, h) for j in range(group) for h in range(n_heads)]
    hs = [slice(h * HEAD_DIM, (h + 1) * HEAD_DIM) for h in range(n_heads)]
    qcol = [qcol_ref[j] for j in range(group)]
    kcol = [kcol_ref[j] for j in range(group)]
    vcol = [vcol_ref[j] for j in range(group)]

    s_own, s_all = {}, {}
    for j, h in chains:
        b = g * group + j
        qb = jnp.broadcast_to(qcol[j][hs[h], :], (HEAD_DIM, LANES))
        s_own[j, h] = jnp.sum(qcol[j][hs[h], :] * kcol[j][hs[h], :], axis=0, keepdims=True)
        rows = []
        for t in range(n_tiles):
            blk = sel_ref[(b * n_heads + h) * MOBA_TOPK + t // pages_per_block]
            pos0 = blk * MOBA_BLOCK + (t % pages_per_block) * LANES
            dist = (past_len - pos0 - lane).astype(F32)
            rows.append(jnp.sum(qb * kbuf_ref[slot, j, h, t], axis=0, keepdims=True) - 2.0 ** -(h + 1) * dist)
        s_all[j, h] = jnp.concatenate(rows, axis=0)
    m = {c: jnp.maximum(s_own[c], jnp.max(s_all[c], axis=(0, 1), keepdims=True)) for c in chains}
    pr = {c: jnp.exp(s_all[c] - m[c]) for c in chains}
    p_own = {c: jnp.exp(s_own[c] - m[c]) for c in chains}
    l = {c: p_own[c] + jnp.sum(pr[c], axis=(0, 1), keepdims=True) for c in chains}
    for j, h in chains:
        acc = p_own[j, h] * vcol[j][hs[h], :]
        wv = None
        for t in range(n_tiles):
            part = vbuf_ref[slot, j, h, t] * pr[j, h][t:t + 1, :]
            wv = part if wv is None else wv + part
        acc = acc + jnp.sum(wv, axis=1, keepdims=True)
        o_ref[j, hs[h], :] = acc * (1.0 / l[j, h])


def _sample_attention(sel_flat, page_table, kc, vc, qcol, kcol, vcol, *, n_heads, past_len):
    bsz = page_table.shape[0]
    d_attn = n_heads * HEAD_DIM
    n_tiles = MOBA_TOPK * (MOBA_BLOCK // LANES)
    group = next(g for g in (4, 2, 1) if bsz % g == 0)
    n_steps = bsz // group
    kern = functools.partial(_sample_attn_kernel, n_steps=n_steps, n_heads=n_heads, group=group,
                             past_len=past_len)
    col_spec = pl.BlockSpec((group, d_attn, 1), lambda g, sel, pt: (g, 0, 0))
    any_spec = pl.BlockSpec(memory_space=pl.ANY)
    grid_spec = pltpu.PrefetchScalarGridSpec(
        num_scalar_prefetch=2,
        grid=(n_steps,),
        in_specs=[any_spec, any_spec, col_spec, col_spec, col_spec],
        out_specs=col_spec,
        scratch_shapes=[
            pltpu.VMEM((2, group, n_heads, n_tiles, HEAD_DIM, LANES), F32),
            pltpu.VMEM((2, group, n_heads, n_tiles, HEAD_DIM, LANES), F32),
            pltpu.SemaphoreType.DMA((2, 2)),
        ],
    )
    return pl.pallas_call(
        kern,
        grid_spec=grid_spec,
        out_shape=jax.ShapeDtypeStruct((bsz, d_attn, 1), F32),
        compiler_params=pltpu.CompilerParams(
            dimension_semantics=("arbitrary",), vmem_limit_bytes=48 * MIB),
        name="sample_moba_attention",
    )(sel_flat, page_table, kc, vc, qcol, kcol, vcol)


def _sample_ffn_kernel(x_ref, att_ref, cm_ref, wo_ref, g1_ref, b1_ref, wup_ref, wfc_ref, sf0_ref, sf1_ref,
                       wdn_ref, g2_ref, b2_ref, y_ref, up_ref, *, d_attn, d_ff, alpha):
    mix = (_dot(att_ref[...].astype(BF16), wo_ref[0:d_attn, :])
           + _dot(cm_ref[...].astype(BF16), wo_ref[d_attn:, :]))
    x1 = _layer_norm(alpha * x_ref[...] + mix, g1_ref[...], b1_ref[...])
    up = _dot(x1.astype(BF16), wup_ref[...])
    up_ref[...] = up
    w = wfc_ref[...]
    conv = w[0:1, :] * sf0_ref[...] + w[1:2, :] * sf1_ref[...] + w[2:3, :] * up
    hcol = (_silu(conv[:, 0:d_ff]) * conv[:, d_ff:]).astype(BF16)
    f = _dot(hcol, wdn_ref[...])
    y_ref[...] = _layer_norm(alpha * x1 + f, g2_ref[...], b2_ref[...])


def _sample_ffn(xs, att, cm, wo, g1, b1, wup, wfc, sf0, sf1, wdn, g2, b2, *, alpha):
    b, d_model = xs.shape
    d_ff = wdn.shape[0]
    kern = functools.partial(_sample_ffn_kernel, d_attn=att.shape[1], d_ff=d_ff, alpha=alpha)
    return pl.pallas_call(
        kern,
        out_shape=[jax.ShapeDtypeStruct((b, d_model), F32), jax.ShapeDtypeStruct((b, 2 * d_ff), F32)],
        compiler_params=pltpu.CompilerParams(vmem_limit_bytes=48 * MIB),
        name="sample_outproj_ffn",
    )(xs, att, cm, wo, g1, b1, wup, wfc, sf0, sf1, wdn, g2, b2)


def kernel(x_prompt, x_sample, cache_k, cache_v, state_conv, state_ffn_conv, page_table, w_in, w_conv, w_out,
           ln1_g, ln1_b, w_ffn_up, w_ffn_conv, w_ffn_down, ln2_g, ln2_b):
    depth = w_in.shape[0]
    assert depth == 1, "single-layer step"
    bsz, seq, d_model = x_prompt.shape
    dec_b, dec_t, _ = x_sample.shape
    assert dec_t == 1
    _, n_phys, page_size, n_heads, head_dim = cache_k.shape
    assert head_dim == HEAD_DIM and page_size == LANES
    d_attn = n_heads * head_dim
    d_conv = (w_in.shape[2] - 3 * d_attn) // 3
    d_ff = w_ffn_down.shape[1]
    past_len = page_table.shape[1] * page_size
    alpha = (2.0 * depth) ** 0.25
    row = lambda a: a.reshape(1, -1)

    w_in_b = w_in[0].astype(BF16)
    wt = w_in_b[:, :3 * d_attn].T
    wr = w_in_b[:, d_attn:2 * d_attn]
    wr = jnp.concatenate([wr, w_in_b[:, 3 * d_attn:]], axis=1)
    wo = w_out[0].astype(BF16)
    wup = w_ffn_up[0].astype(BF16)
    wdn = w_ffn_down[0].astype(BF16)
    wc, wfc = w_conv[0], w_ffn_conv[0]
    g1, b1, g2, b2 = row(ln1_g[0]), row(ln1_b[0]), row(ln2_g[0]), row(ln2_b[0])

    xs = x_sample.reshape(dec_b, d_model)
    st0, st1 = state_conv[0, :, 0, :], state_conv[0, :, 1, :]
    q_s, k_s, v_s, cm_s, u_s = _sample_inproj(xs, w_in[0], wc, st0, st1, d_attn=d_attn, d_conv=d_conv)
    kc = jnp.transpose(cache_k[0], (0, 2, 3, 1)).reshape(n_phys, d_attn, page_size)
    vc = jnp.transpose(cache_v[0], (0, 2, 3, 1)).reshape(n_phys, d_attn, page_size)
    qcol, kcol, vcol = q_s[:, :, None], k_s[:, :, None], v_s[:, :, None]

    x2d = x_prompt.reshape(bsz * seq, d_model)
    qt, kt, vt, krm, kmean, cm, conv_p = _prompt_inproj(x2d, wt, wr, wc, bsz=bsz, seq=seq, tm=1024)
    att, sel = _prompt_attention_and_sample_gate(qt, krm, vt, kmean, page_table, kc, qcol, kcol,
                                                 bsz=bsz, seq=seq, past_len=past_len, pages_per_wave=32)
    y_p, ffn_p = _prompt_ffn(x2d, att, cm, wo, g1, b1, wup, wfc, wdn, g2, b2,
                             bsz=bsz, seq=seq, tm=512, cw=256, alpha=alpha)
    to_cache = lambda a: jnp.transpose(a.reshape(1, bsz, n_heads, head_dim, seq), (0, 1, 4, 2, 3))
    k_prompt, v_prompt = to_cache(kt), to_cache(vt)

    sel_flat = jnp.transpose(sel[:, :MOBA_TOPK, :n_heads], (0, 2, 1)).reshape(-1)
    att_s = _sample_attention(sel_flat, page_table, kc, vc, qcol, kcol, vcol,
                              n_heads=n_heads, past_len=past_len)
    sf0, sf1 = state_ffn_conv[0, :, 0, :], state_ffn_conv[0, :, 1, :]
    y_s, up_s = _sample_ffn(xs, att_s.reshape(dec_b, d_attn), cm_s, wo, g1, b1, wup, wfc, sf0, sf1,
                            wdn, g2, b2, alpha=alpha)

    return (y_p.reshape(bsz, seq, d_model),
            y_s.reshape(dec_b, 1, d_model),
            k_prompt, v_prompt,
            conv_p[None], ffn_p[None],
            k_s.reshape(1, dec_b, 1, n_heads, head_dim), v_s.reshape(1, dec_b, 1, n_heads, head_dim),
            jnp.stack([st1, u_s], axis=1)[None],
            jnp.stack([sf1, up_s], axis=1)[None])
```
